```python
import functools
import jax, jax.numpy as jnp
from jax import lax
import numpy as np

D_MODEL = 1024
BATCH = 8
SEQ = 8192
DEPTH = 1
DEC_BATCH = 128
DEC_SEQ = 1
PAST_LEN = 8192
PAGE_SIZE = 128

N_HEADS = 8
HEAD_DIM = 64
D_ATTN = N_HEADS * HEAD_DIM
D_CONV = 512
CONV_WIDTH = 3
MOBA_BLOCK = 256
MOBA_TOPK = 3
D_FF = -(-8 * D_MODEL // (3 * 256)) * 256
D_PLE = 256
RMS_EPS = 1e-6
NEG_INF = -1e30
SPLITS = (D_ATTN, 2 * D_ATTN, 3 * D_ATTN, 3 * D_ATTN + D_CONV, 3 * D_ATTN + 2 * D_CONV,
          3 * D_ATTN + 3 * D_CONV, 3 * D_ATTN + 3 * D_CONV + D_MODEL)
D_IN_TOTAL = 3 * D_ATTN + 3 * D_CONV + 2 * D_MODEL

kernel_name = "moba_shortconv_gated_hybrid_step"


def rms_norm(x, w):
    x32 = x.astype(jnp.float32)
    y = x32 * lax.rsqrt(jnp.mean(x32 * x32, axis=-1, keepdims=True) + RMS_EPS)
    return (y * w.astype(jnp.float32)).astype(x.dtype)


def alibi_slopes():
    return 2.0 ** (-8.0 * jnp.arange(1, N_HEADS + 1, dtype=jnp.float32) / N_HEADS)


def short_conv(u_ext, conv_w):
    L = u_ext.shape[1] - (CONV_WIDTH - 1)
    out = conv_w[0] * u_ext[:, 0:L]
    for j in range(1, CONV_WIDTH):
        out = out + conv_w[j] * u_ext[:, j:j + L]
    return out


def moba_prompt(q, k, v, slopes):
    B, S = q.shape[0], q.shape[1]
    nb = -(-S // MOBA_BLOCK)
    sp = nb * MOBA_BLOCK
    pad = [(0, 0), (0, sp - S), (0, 0), (0, 0)]
    qt = jnp.pad(q, pad).reshape(B, nb, MOBA_BLOCK, N_HEADS, HEAD_DIM).transpose(0, 1, 3, 2, 4)
    kt = jnp.pad(k, pad).reshape(B, nb, MOBA_BLOCK, N_HEADS, HEAD_DIM).transpose(0, 3, 1, 2, 4)
    vt = jnp.pad(v, pad).reshape(B, nb, MOBA_BLOCK, N_HEADS, HEAD_DIM).transpose(0, 3, 1, 2, 4)
    kbar = jnp.mean(kt.astype(jnp.float32), axis=3)
    topk = min(MOBA_TOPK, nb)
    offs = jnp.arange(MOBA_BLOCK)
    h_idx = jnp.arange(N_HEADS)[:, None, None]
    scale = HEAD_DIM ** -0.5

    def one_block(idx):
        b = idx // nb
        j = idx % nb
        qb = qt[b, j]
        kb = kt[b]
        vb = vt[b]
        gate = jnp.einsum('hqd,hnd->hqn', qb.astype(jnp.float32), kbar[b])
        gate = jnp.where(jnp.arange(nb) < j, gate, NEG_INF)
        _, sel = lax.top_k(gate, topk)
        sel_ok = jnp.arange(topk) < j
        ksel = kb[h_idx, sel]
        vsel = vb[h_idx, sel]
        q_pos = j * MOBA_BLOCK + offs
        sel_pos = sel[..., None] * MOBA_BLOCK + offs
        s_sel = (jnp.einsum('hqd,hqnkd->hqnk', qb, ksel, preferred_element_type=jnp.float32) * scale
                 - slopes[:, None, None, None] * (q_pos[None, :, None, None] - sel_pos).astype(jnp.float32))
        s_sel = jnp.where(sel_ok[None, None, :, None], s_sel, NEG_INF)
        kown = kb[:, j]
        vown = vb[:, j]
        s_own = (jnp.einsum('hqd,hkd->hqk', qb, kown, preferred_element_type=jnp.float32) * scale
                 - slopes[:, None, None] * (q_pos[:, None] - q_pos[None, :]).astype(jnp.float32))
        s_own = jnp.where((q_pos[None, :] <= q_pos[:, None])[None], s_own, NEG_INF)
        s = jnp.concatenate([s_sel.reshape(N_HEADS, MOBA_BLOCK, topk * MOBA_BLOCK), s_own], axis=-1)
        p = jax.nn.softmax(s, axis=-1)
        p_sel = p[..., :topk * MOBA_BLOCK].reshape(N_HEADS, MOBA_BLOCK, topk, MOBA_BLOCK).astype(v.dtype)
        p_own = p[..., topk * MOBA_BLOCK:].astype(v.dtype)
        o = jnp.einsum('hqnk,hqnkd->hqd', p_sel, vsel) + jnp.einsum('hqk,hkd->hqd', p_own, vown)
        return o.transpose(1, 0, 2)

    out = lax.map(one_block, jnp.arange(B * nb))
    return out.reshape(B, sp, N_HEADS, HEAD_DIM)[:, :S]


def gather_rows(cache, new_rows, page_table, pos, b_idx, h_idx):
    past = page_table.shape[1] * PAGE_SIZE
    ds = new_rows.shape[1]
    pc = jnp.minimum(pos, past - 1)
    phys = page_table[b_idx, pc // PAGE_SIZE]
    from_cache = cache[phys, pc % PAGE_SIZE, h_idx]
    from_new = new_rows[b_idx, jnp.clip(pos - past, 0, ds - 1), h_idx]
    return jnp.where((pos < past)[..., None], from_cache, from_new)


def moba_sample(q, k, v, slopes, cache_k, cache_v, page_table):
    DB, DS = q.shape[0], q.shape[1]
    past = page_table.shape[1] * PAGE_SIZE
    L = past + DS
    nb = -(-L // MOBA_BLOCK)
    lp = nb * MOBA_BLOCK
    past_k = cache_k[page_table].reshape(DB, past, N_HEADS, HEAD_DIM)
    k_all = jnp.pad(jnp.concatenate([past_k, k], axis=1), [(0, 0), (0, lp - L), (0, 0), (0, 0)])
    kt = k_all.reshape(DB, nb, MOBA_BLOCK, N_HEADS, HEAD_DIM).transpose(0, 3, 1, 2, 4)
    kbar = jnp.mean(kt.astype(jnp.float32), axis=3)
    q_pos = past + jnp.arange(DS)
    jq = q_pos // MOBA_BLOCK
    qh = q.transpose(0, 2, 1, 3)
    gate = jnp.einsum('bhqd,bhnd->bhqn', qh.astype(jnp.float32), kbar)
    gate = jnp.where(jnp.arange(nb)[None, None, None, :] < jq[None, None, :, None], gate, NEG_INF)
    topk = min(MOBA_TOPK, nb)
    _, sel = lax.top_k(gate, topk)
    sel_ok = jnp.arange(topk)[None, :] < jq[:, None]
    offs = jnp.arange(MOBA_BLOCK)
    b4 = jnp.arange(DB)[:, None, None, None]
    h4 = jnp.arange(N_HEADS)[None, :, None, None]
    ksel = kt[b4, h4, sel]
    sel_pos = sel[..., None] * MOBA_BLOCK + offs
    vsel = gather_rows(cache_v, v, page_table, sel_pos, b4[..., None], h4[..., None])
    own_pos = jq[:, None] * MOBA_BLOCK + offs
    b3 = jnp.arange(DB)[:, None, None]
    h3 = jnp.arange(N_HEADS)[None, :, None]
    kown = kt[b3, h3, jq[None, None, :]]
    vown = gather_rows(cache_v, v, page_table, own_pos[None, None], b4, h4)
    scale = HEAD_DIM ** -0.5
    s_sel = (jnp.einsum('bhqd,bhqnkd->bhqnk', qh, ksel, preferred_element_type=jnp.float32) * scale
             - slopes[None, :, None, None, None] * (q_pos[None, None, :, None, None] - sel_pos).astype(jnp.float32))
    s_sel = jnp.where(sel_ok[None, None, :, :, None], s_sel, NEG_INF)
    s_own = (jnp.einsum('bhqd,bhqkd->bhqk', qh, kown, preferred_element_type=jnp.float32) * scale
             - slopes[None, :, None, None] * (q_pos[:, None] - own_pos)[None, None].astype(jnp.float32))
    s_own = jnp.where((own_pos <= q_pos[:, None])[None, None], s_own, NEG_INF)
    s = jnp.concatenate([s_sel.reshape(DB, N_HEADS, DS, topk * MOBA_BLOCK), s_own], axis=-1)
    p = jax.nn.softmax(s, axis=-1)
    p_sel = p[..., :topk * MOBA_BLOCK].reshape(DB, N_HEADS, DS, topk, MOBA_BLOCK).astype(v.dtype)
    p_own = p[..., topk * MOBA_BLOCK:].astype(v.dtype)
    o = jnp.einsum('bhqnk,bhqnkd->bhqd', p_sel, vsel) + jnp.einsum('bhqk,bhqkd->bhqd', p_own, vown)
    return o.transpose(0, 2, 1, 3)


def decoder_layer(x, p_emb, conv_hist, attend, norm1, w_in, q_norm, k_norm, conv_w, w_attn_out,
                  w_conv_out, w_o, norm2, w_gate, w_up, w_down, norm3, w_ple, w_ple_gate):
    N, L = x.shape[0], x.shape[1]
    h = rms_norm(x, norm1)
    z = h @ w_in
    q, k, v, cb, cc, cx, ga, gc = jnp.split(z, SPLITS, axis=-1)
    hs = (N, L, N_HEADS, HEAD_DIM)
    q = rms_norm(q.reshape(hs), q_norm)
    k = rms_norm(k.reshape(hs), k_norm)
    v = v.reshape(hs)
    y_attn = attend(q, k, v).reshape(N, L, D_ATTN) @ w_attn_out
    u_ext = jnp.concatenate([conv_hist, cc * cx], axis=1)
    y_conv = (cb * short_conv(u_ext, conv_w)) @ w_conv_out
    merged = jax.nn.sigmoid(ga) * y_attn + jax.nn.sigmoid(gc) * y_conv
    x = x + merged @ w_o
    h2 = rms_norm(x, norm2)
    x = x + (jax.nn.silu(h2 @ w_gate) * (h2 @ w_up)) @ w_down
    h3 = rms_norm(x, norm3)
    x = x + (p_emb @ w_ple) * jax.nn.sigmoid(h3 @ w_ple_gate)
    return x, k, v, u_ext[:, -(CONV_WIDTH - 1):]


def setup_inputs(seed: int = 0) -> dict:
    key = jax.random.key(seed)
    ks = jax.random.split(key, 24)
    f32 = jnp.float32
    n_pages = PAST_LEN // PAGE_SIZE
    n_used = DEC_BATCH * n_pages
    n_phys = n_used + n_used // 4
    nrm = lambda k, shape, s: jax.random.normal(k, shape, f32) * s
    gain = lambda k, shape: 1.0 + 0.02 * jax.random.normal(k, shape, f32)
    page_table = jax.random.permutation(ks[7], n_phys)[:n_used].reshape(DEC_BATCH, n_pages).astype(jnp.int32)
    return {
        "x_prompt": nrm(ks[0], (BATCH, SEQ, D_MODEL), 1.0),
        "x_sample": nrm(ks[1], (DEC_BATCH, DEC_SEQ, D_MODEL), 1.0),
        "p_prompt": nrm(ks[2], (DEPTH, BATCH, SEQ, D_PLE), 1.0),
        "p_sample": nrm(ks[3], (DEPTH, DEC_BATCH, DEC_SEQ, D_PLE), 1.0),
        "cache_k": nrm(ks[4], (DEPTH, n_phys, PAGE_SIZE, N_HEADS, HEAD_DIM), 1.0),
        "cache_v": nrm(ks[5], (DEPTH, n_phys, PAGE_SIZE, N_HEADS, HEAD_DIM), 1.0),
        "state_conv": nrm(ks[6], (DEPTH, DEC_BATCH, CONV_WIDTH - 1, D_CONV), 1.0),
        "page_table": page_table,
        "norm1": gain(ks[8], (DEPTH, D_MODEL)),
        "w_in": nrm(ks[9], (DEPTH, D_MODEL, D_IN_TOTAL), D_MODEL ** -0.5),
        "q_norm": gain(ks[10], (DEPTH, HEAD_DIM)),
        "k_norm": gain(ks[11], (DEPTH, HEAD_DIM)),
        "conv_w": nrm(ks[12], (DEPTH, CONV_WIDTH, D_CONV), CONV_WIDTH ** -0.5),
        "w_attn_out": nrm(ks[13], (DEPTH, D_ATTN, D_MODEL), D_ATTN ** -0.5),
        "w_conv_out": nrm(ks[14], (DEPTH, D_CONV, D_MODEL), D_CONV ** -0.5),
        "w_o": nrm(ks[15], (DEPTH, D_MODEL, D_MODEL), D_MODEL ** -0.5),
        "norm2": gain(ks[16], (DEPTH, D_MODEL)),
        "w_gate": nrm(ks[17], (DEPTH, D_MODEL, D_FF), D_MODEL ** -0.5),
        "w_up": nrm(ks[18], (DEPTH, D_MODEL, D_FF), D_MODEL ** -0.5),
        "w_down": nrm(ks[19], (DEPTH, D_FF, D_MODEL), D_FF ** -0.5),
        "norm3": gain(ks[20], (DEPTH, D_MODEL)),
        "w_ple": nrm(ks[21], (DEPTH, D_PLE, D_MODEL), D_PLE ** -0.5),
        "w_ple_gate": nrm(ks[22], (DEPTH, D_MODEL, D_MODEL), D_MODEL ** -0.5),
    }


def reference(x_prompt, x_sample, p_prompt, p_sample, cache_k, cache_v, state_conv, page_table,
              norm1, w_in, q_norm, k_norm, conv_w, w_attn_out, w_conv_out, w_o, norm2, w_gate,
              w_up, w_down, norm3, w_ple, w_ple_gate):
    slopes = alibi_slopes()
    yp, ys = x_prompt, x_sample
    kp_l, vp_l, cp_l, ks_l, vs_l, cs_l = [], [], [], [], [], []
    for i in range(DEPTH):
        w = (norm1[i], w_in[i], q_norm[i], k_norm[i], conv_w[i], w_attn_out[i], w_conv_out[i], w_o[i],
             norm2[i], w_gate[i], w_up[i], w_down[i], norm3[i], w_ple[i], w_ple_gate[i])
        hist0 = jnp.zeros((yp.shape[0], CONV_WIDTH - 1, D_CONV), yp.dtype)
        attend_p = functools.partial(moba_prompt, slopes=slopes)
        yp, kp, vp, cp = decoder_layer(yp, p_prompt[i], hist0, attend_p, *w)
        attend_s = functools.partial(moba_sample, slopes=slopes, cache_k=cache_k[i],
                                     cache_v=cache_v[i], page_table=page_table)
        ys, kn, vn, cn = decoder_layer(ys, p_sample[i], state_conv[i], attend_s, *w)
        kp_l.append(kp); vp_l.append(vp); cp_l.append(cp)
        ks_l.append(kn); vs_l.append(vn); cs_l.append(cn)
    k_prompt = jnp.stack(kp_l)
    v_prompt = jnp.stack(vp_l)
    conv_prompt = jnp.stack(cp_l)
    k_sample = jnp.stack(ks_l)
    v_sample = jnp.stack(vs_l)
    conv_sample = jnp.stack(cs_l)
    return (yp, ys, k_prompt, v_prompt, conv_prompt, k_sample, v_sample, conv_sample)
```

```python
import functools

import jax
import jax.numpy as jnp
from jax import lax
from jax.experimental import pallas as pl
from jax.experimental.pallas import tpu as pltpu

RMS_EPS = 1e-6
NEG_INF = -1e30
BELOW_NEG_INF = -3e38
MOBA_BLOCK = 256
MOBA_TOPK = 3
CONV_WIDTH = 3
HEADS_PER_STEP = 2
SUBLANES = 8
VMEM_LIMIT_BYTES = 60 * 1024 * 1024
ROW_TILE = 512
PAGES_PER_CHUNK = 16

F32 = jnp.float32
BF16 = jnp.bfloat16


def _dot(a, b):
    return jnp.dot(a, b, preferred_element_type=F32)


def _dot_nt(a, b):
    return lax.dot_general(a, b, (((1,), (1,)), ((), ())), preferred_element_type=F32)


def _split_bf16(a):
    hi = a.astype(BF16)
    lo = (a - hi.astype(F32)).astype(BF16)
    return hi, lo


def _dot_3pass(a, b):
    ah, al = _split_bf16(a)
    bh, bl = _split_bf16(b)
    return _dot(ah, bh) + (_dot(ah, bl) + _dot(al, bh))


def _rms_norm(x, w):
    ms = jnp.mean(x * x, axis=-1, keepdims=True)
    return x * lax.rsqrt(ms + RMS_EPS) * w


def _head_norm_t(z_t, w_col, n_heads, head_dim):
    tokens = z_t.shape[1]
    z3 = z_t.reshape(n_heads, head_dim, tokens)
    ms = jnp.mean(z3 * z3, axis=1, keepdims=True)
    y = (z3 * lax.rsqrt(ms + RMS_EPS)).reshape(n_heads * head_dim, tokens)
    return y * w_col


def _const_spec(shape):
    zeros = (0,) * len(shape)
    return pl.BlockSpec(shape, lambda *_: zeros, pipeline_mode=pl.Buffered(1))


def _proj_kernel(*refs, seq_mode, tm, n_heads, head_dim, d_attn, d_conv, d_model):
    if seq_mode:
        (x_ref, n1_ref, wqkv_t_ref, wrest_ref, qn_ref, kn_ref, convw_ref, wco_ref,
         q_t_ref, k_t_ref, v_t_ref, v_tb_ref, k_rm_ref, kbar_ref, gconv_ref, siga_ref, tail_ref,
         ubuf_ref) = refs
    else:
        (x_ref, h0_ref, h1_ref, n1_ref, wqkv_t_ref, wrest_ref, qn_ref, kn_ref, convw_ref, wco_ref,
         q_t_ref, k_t_ref, v_t_ref, sself_ref, gconv_ref, siga_ref, u_ref) = refs

    x = x_ref[0]
    h = _rms_norm(x, n1_ref[...]).astype(BF16)

    q_t = _head_norm_t(_dot_nt(wqkv_t_ref[0:d_attn, :], h), qn_ref[...], n_heads, head_dim)
    k_t = _head_norm_t(_dot_nt(wqkv_t_ref[d_attn:2 * d_attn, :], h), kn_ref[...], n_heads, head_dim)
    v_t = _dot_nt(wqkv_t_ref[2 * d_attn:3 * d_attn, :], h)
    q_t_ref[0] = q_t
    k_t_ref[0] = k_t
    v_t_ref[0] = v_t

    if seq_mode:
        k_rm = k_t.T
        k_rm_ref[0] = k_rm.astype(BF16)
        for i in range(tm // MOBA_BLOCK):
            lo = i * MOBA_BLOCK
            v_tb_ref[0, i] = v_t[:, lo:lo + MOBA_BLOCK].astype(BF16)
            kbar_ref[0, i:i + 1, :] = jnp.mean(k_rm[lo:lo + MOBA_BLOCK], axis=0, keepdims=True)
    else:
        qk = (q_t * k_t).reshape(n_heads, head_dim, tm)
        sself_ref[...] = jnp.sum(qk, axis=1)

    zc = _dot(h, wrest_ref[:, 0:3 * d_conv])
    cb = zc[:, 0:d_conv]
    u = zc[:, d_conv:2 * d_conv] * zc[:, 2 * d_conv:3 * d_conv]
    if seq_mode:
        @pl.when(pl.program_id(1) == 0)
        def _():
            ubuf_ref[0:SUBLANES, :] = jnp.zeros((SUBLANES, d_conv), F32)
        ubuf_ref[SUBLANES:SUBLANES + tm, :] = u
        u_m2 = ubuf_ref[SUBLANES - 2:SUBLANES - 2 + tm, :]
        u_m1 = ubuf_ref[SUBLANES - 1:SUBLANES - 1 + tm, :]
        tail_ref[0] = u[tm - (CONV_WIDTH - 1):tm, :]
        ubuf_ref[0:SUBLANES, :] = u[tm - SUBLANES:tm, :]
    else:
        u_m2 = h0_ref[...]
        u_m1 = h1_ref[...]
        u_ref[...] = u
    conv = convw_ref[0:1, :] * u_m2 + convw_ref[1:2, :] * u_m1 + convw_ref[2:3, :] * u
    y_conv = _dot((cb * conv).astype(BF16), wco_ref[...])

    zg = _dot(h, wrest_ref[:, 3 * d_conv:3 * d_conv + 2 * d_model])
    siga_ref[...] = jax.nn.sigmoid(zg[:, 0:d_model]).astype(BF16)
    gconv_ref[...] = (jax.nn.sigmoid(zg[:, d_model:2 * d_model]) * y_conv).astype(BF16)


def _proj_call(x3, hist, weights, *, seq_mode, tm):
    (n1, wqkv_t, wrest, qn_col, kn_col, convw, wco, n_heads, head_dim) = weights
    n_seq, seq_len, d_model = x3.shape
    d_attn = n_heads * head_dim
    d_conv = convw.shape[1]
    n_tiles = seq_len // tm
    rows = n_seq * seq_len
    kern = functools.partial(_proj_kernel, seq_mode=seq_mode, tm=tm, n_heads=n_heads,
                             head_dim=head_dim, d_attn=d_attn, d_conv=d_conv, d_model=d_model)
    row_map = lambda b, t: (b * n_tiles + t, 0)
    w_specs = [_const_spec(n1.shape), _const_spec(wqkv_t.shape), _const_spec(wrest.shape),
               _const_spec(qn_col.shape), _const_spec(kn_col.shape), _const_spec(convw.shape),
               _const_spec(wco.shape)]
    t_spec = pl.BlockSpec((1, d_attn, tm), lambda b, t: (b, 0, t))
    t_shape = jax.ShapeDtypeStruct((n_seq, d_attn, seq_len), F32)
    gate_specs = [pl.BlockSpec((tm, d_model), row_map)] * 2
    gate_shapes = [jax.ShapeDtypeStruct((rows, d_model), BF16)] * 2
    x_spec = pl.BlockSpec((1, tm, d_model), lambda b, t: (b, t, 0))
    if seq_mode:
        nb_t = tm // MOBA_BLOCK
        in_specs = [x_spec] + w_specs
        args = (x3,)
        out_specs = [t_spec, t_spec, t_spec,
                     pl.BlockSpec((1, nb_t, d_attn, MOBA_BLOCK), lambda b, t: (b, t, 0, 0)),
                     pl.BlockSpec((1, tm, d_attn), lambda b, t: (b, t, 0)),
                     pl.BlockSpec((1, nb_t, d_attn), lambda b, t: (b * n_tiles + t, 0, 0)),
                     *gate_specs,
                     pl.BlockSpec((1, CONV_WIDTH - 1, d_conv), lambda b, t: (b * n_tiles + t, 0, 0))]
        out_shape = [t_shape, t_shape, t_shape,
                     jax.ShapeDtypeStruct((n_seq, seq_len // MOBA_BLOCK, d_attn, MOBA_BLOCK), BF16),
                     jax.ShapeDtypeStruct((n_seq, seq_len, d_attn), BF16),
                     jax.ShapeDtypeStruct((n_seq * n_tiles, nb_t, d_attn), F32),
                     *gate_shapes,
                     jax.ShapeDtypeStruct((n_seq * n_tiles, CONV_WIDTH - 1, d_conv), F32)]
        scratch = [pltpu.VMEM((SUBLANES + tm, d_conv), F32)]
    else:
        h_spec = pl.BlockSpec((tm, d_conv), row_map)
        in_specs = [x_spec, h_spec, h_spec] + w_specs
        args = (x3, hist[0], hist[1])
        out_specs = [t_spec, t_spec, t_spec,
                     pl.BlockSpec((n_heads, tm), lambda b, t: (0, b * n_tiles + t)),
                     *gate_specs,
                     pl.BlockSpec((tm, d_conv), row_map)]
        out_shape = [t_shape, t_shape, t_shape,
                     jax.ShapeDtypeStruct((n_heads, rows), F32),
                     *gate_shapes,
                     jax.ShapeDtypeStruct((rows, d_conv), F32)]
        scratch = []
    return pl.pallas_call(
        kern, grid=(n_seq, n_tiles), in_specs=in_specs, out_specs=out_specs, out_shape=out_shape,
        scratch_shapes=scratch,
        compiler_params=pltpu.CompilerParams(dimension_semantics=("arbitrary", "arbitrary"),
                                             vmem_limit_bytes=VMEM_LIMIT_BYTES),
        name="proj_seq" if seq_mode else "proj_rows",
    )(*args, n1, wqkv_t, wrest, qn_col, kn_col, convw, wco)


def _topk_rows(gate, row_iota, k):
    n = gate.shape[0]
    picks = []
    for _ in range(k):
        mx = jnp.max(gate, axis=0, keepdims=True)
        idx = jnp.min(jnp.where(gate == mx, row_iota, n), axis=0, keepdims=True)
        picks.append(idx)
        gate = jnp.where(row_iota == idx, BELOW_NEG_INF, gate)
    return picks


def _moba_prompt_kernel(slopes_ref, q_t_ref, k_ref, v_tb_ref, kbar_ref, o_ref,
                        qm_ref, brel_ref, rb_ref, *, head_dim, n_blocks):
    hp = pl.program_id(1)
    j = pl.program_id(2)
    blk = MOBA_BLOCK
    hw = HEADS_PER_STEP * head_dim
    q2 = q_t_ref[0]
    key_i = lax.broadcasted_iota(jnp.int32, (blk, blk), 0)
    qry_i = lax.broadcasted_iota(jnp.int32, (blk, blk), 1)
    rel = (qry_i - key_i).astype(F32)
    causal = key_i <= qry_i
    blk_i = lax.broadcasted_iota(jnp.int32, (n_blocks, blk), 0)
    past = blk_i < j
    chan_i = lax.broadcasted_iota(jnp.int32, (hw, blk), 0)
    scale = head_dim ** -0.5
    k_own = k_ref[0, pl.ds(pl.multiple_of(j * blk, blk), blk), :]

    state = []
    for hh in range(HEADS_PER_STEP):
        slope = slopes_ref[hp * HEADS_PER_STEP + hh]
        qh = q2[hh * head_dim:(hh + 1) * head_dim]
        gate = _dot_3pass(kbar_ref[0, hh], qh)
        gate = jnp.where(past, gate, NEG_INF)
        rb = jnp.full((n_blocks, blk), NEG_INF, F32)
        block_bias = (-slope * blk) * (j - blk_i).astype(F32)
        for idx in _topk_rows(gate, blk_i, MOBA_TOPK):
            rb = jnp.where((blk_i == idx) & past, block_bias, rb)
        rb_ref[hh] = rb
        in_head = (chan_i >= hh * head_dim) & (chan_i < (hh + 1) * head_dim)
        qm = jnp.where(in_head, q2 * scale, 0.0).astype(BF16)
        qm_ref[hh] = qm
        brel = -slope * rel
        brel_ref[hh] = brel
        s = _dot(k_own, qm) + jnp.where(causal, brel, NEG_INF)
        m = jnp.max(s, axis=0, keepdims=True)
        p = jnp.exp(s - m)
        l = jnp.sum(p, axis=0, keepdims=True)
        acc = _dot(v_tb_ref[0, j, hh * head_dim:(hh + 1) * head_dim, :], p.astype(BF16))
        state.append((m, l, acc))

    def body(n, carry):
        k_blk = k_ref[0, pl.ds(pl.multiple_of(n * blk, blk), blk), :]
        new = []
        for hh in range(HEADS_PER_STEP):
            m, l, acc = carry[hh]
            s = _dot(k_blk, qm_ref[hh]) + brel_ref[hh] + rb_ref[hh, pl.ds(n, 1), :]
            m_new = jnp.maximum(m, jnp.max(s, axis=0, keepdims=True))
            alpha = jnp.exp(m - m_new)
            p = jnp.exp(s - m_new)
            l = alpha * l + jnp.sum(p, axis=0, keepdims=True)
            pv = _dot(v_tb_ref[0, n, hh * head_dim:(hh + 1) * head_dim, :], p.astype(BF16))
            new.append((m_new, l, alpha * acc + pv))
        return tuple(new)

    state = lax.fori_loop(0, j, body, tuple(state))
    o_t = jnp.concatenate([acc / l for (_, l, acc) in state], axis=0)
    o_ref[0] = o_t.T.astype(o_ref.dtype)


def _moba_prompt_call(slopes, q_t, k_rm, v_tb, kbar, *, n_heads, head_dim):
    n_seq, d_attn, seq_len = q_t.shape
    n_blocks = seq_len // MOBA_BLOCK
    hw = HEADS_PER_STEP * head_dim
    kern = functools.partial(_moba_prompt_kernel, head_dim=head_dim, n_blocks=n_blocks)
    return pl.pallas_call(
        kern,
        grid=(n_seq, n_heads // HEADS_PER_STEP, n_blocks),
        in_specs=[
            pl.BlockSpec(memory_space=pltpu.SMEM),
            pl.BlockSpec((1, hw, MOBA_BLOCK), lambda b, hp, j: (b, hp, j)),
            pl.BlockSpec((1, seq_len, hw), lambda b, hp, j: (b, 0, hp)),
            pl.BlockSpec((1, n_blocks, hw, MOBA_BLOCK), lambda b, hp, j: (b, 0, hp, 0)),
            pl.BlockSpec((1, HEADS_PER_STEP, n_blocks, head_dim), lambda b, hp, j: (b, hp, 0, 0)),
        ],
        out_specs=pl.BlockSpec((1, MOBA_BLOCK, hw), lambda b, hp, j: (b, j, hp)),
        out_shape=jax.ShapeDtypeStruct((n_seq, seq_len, d_attn), BF16),
        scratch_shapes=[pltpu.VMEM((HEADS_PER_STEP, hw, MOBA_BLOCK), BF16),
                        pltpu.VMEM((HEADS_PER_STEP, MOBA_BLOCK, MOBA_BLOCK), F32),
                        pltpu.VMEM((HEADS_PER_STEP, n_blocks, MOBA_BLOCK), F32)],
        compiler_params=pltpu.CompilerParams(
            dimension_semantics=("arbitrary", "arbitrary", "arbitrary"),
            vmem_limit_bytes=VMEM_LIMIT_BYTES),
        name="moba_prompt",
    )(slopes, q_t, k_rm, v_tb, kbar)


def _sample_scores_kernel(pt_ref, qb_ref, ck_hbm, sc_ref, buf_ref, sem_ref, *, ch, n_heads, head_dim):
    b = pl.program_id(0)
    c = pl.program_id(1)
    nc = pl.num_programs(1)
    step = b * nc + c
    total = pl.num_programs(0) * nc

    def page_copy(seq, chunk, i, slot):
        page = pt_ref[seq, chunk * ch + i]
        return pltpu.make_async_copy(ck_hbm.at[page], buf_ref.at[slot, i], sem_ref.at[slot])

    def start(next_step):
        seq = next_step // nc
        chunk = next_step % nc
        for i in range(ch):
            page_copy(seq, chunk, i, next_step % 2).start()

    @pl.when(step == 0)
    def _():
        start(step)

    @pl.when(step + 1 < total)
    def _():
        start(step + 1)

    slot = step % 2
    for i in range(ch):
        page_copy(b, c, i, slot).wait()

    def body(i, carry):
        rows = []
        for h in range(n_heads):
            prod = buf_ref[slot, i, h] * qb_ref[0, h * head_dim:(h + 1) * head_dim, :]
            rows.append(jnp.sum(prod, axis=0, keepdims=True))
        sc_ref[0, i] = jnp.concatenate(rows, axis=0)
        return carry

    lax.fori_loop(0, ch, body, 0)


def _sample_probs_kernel(sc_ref, sself_ref, slopes_ref, p_ref, pself_ref, sel_ref,
                         *, n_pages, page_size, past_len, head_dim, n_heads):
    pages_per_block = MOBA_BLOCK // page_size
    n_blocks = n_pages // pages_per_block
    lanes = sself_ref.shape[-1]
    lane_i = lax.broadcasted_iota(jnp.int32, (n_heads, lanes), 1)
    gate = jnp.full((n_heads, lanes), NEG_INF, F32)
    for n in range(n_blocks):
        tot = sc_ref[0, n * pages_per_block]
        for pg in range(1, pages_per_block):
            tot = tot + sc_ref[0, n * pages_per_block + pg]
        g_n = jnp.sum(tot, axis=1, keepdims=True) * (1.0 / MOBA_BLOCK)
        gate = jnp.where(lane_i == n, g_n, gate)
    picks = []
    for _ in range(MOBA_TOPK):
        mx = jnp.max(gate, axis=1, keepdims=True)
        idx = jnp.min(jnp.where(gate == mx, lane_i, lanes), axis=1, keepdims=True)
        picks.append(idx)
        gate = jnp.where(lane_i == idx, BELOW_NEG_INF, gate)
    sel_vec = jnp.zeros((n_heads, lanes), jnp.int32)
    for t, idx in enumerate(picks):
        sel_vec = jnp.where(lane_i == t, idx, sel_vec)
    sel_ref[0] = sel_vec

    scale = head_dim ** -0.5
    slope = slopes_ref[...]
    pos_in_page = lax.broadcasted_iota(jnp.int32, (n_heads, page_size), 1)
    s_self = sself_ref[0] * scale

    def page_scores(pg):
        selected = picks[0] == (pg // pages_per_block)
        for idx in picks[1:]:
            selected = selected | (idx == (pg // pages_per_block))
        dist = (past_len - pg * page_size - pos_in_page).astype(F32)
        s = sc_ref[0, pg] * scale - slope[:, 0:page_size] * dist
        return jnp.where(selected, s, NEG_INF)

    m = s_self
    for pg in range(n_pages):
        m = jnp.maximum(m, jnp.max(page_scores(pg), axis=1, keepdims=True))
    e_self = jnp.exp(s_self - m)
    l = e_self
    for pg in range(n_pages):
        e = jnp.exp(page_scores(pg) - m[:, 0:page_size])
        p_ref[0, pg] = e
        l = l + jnp.sum(e, axis=1, keepdims=True)
    inv = 1.0 / l
    for pg in range(n_pages):
        p_ref[0, pg] = p_ref[0, pg] * inv[:, 0:page_size]
    pself_ref[0] = e_self * inv


def _sample_pv_kernel(pt_ref, sel_ref, p_ref, pself_e_ref, v_t_ref, cv_hbm, o_ref, vbuf_ref, sem_ref,
                      *, n_heads, head_dim, pages_per_block):
    b = pl.program_id(0)
    nb = pl.num_programs(0)
    per_head = MOBA_TOPK * pages_per_block

    def tile_copy(seq, h, t, pg, slot):
        blk = sel_ref[seq, h * MOBA_TOPK + t]
        page = pt_ref[seq, blk * pages_per_block + pg]
        k = h * per_head + t * pages_per_block + pg
        return pltpu.make_async_copy(cv_hbm.at[page, h], vbuf_ref.at[slot, k], sem_ref.at[slot])

    def for_all_tiles(seq, slot, fn):
        for h in range(n_heads):
            for t in range(MOBA_TOPK):
                for pg in range(pages_per_block):
                    fn(tile_copy(seq, h, t, pg, slot))

    @pl.when(b == 0)
    def _():
        for_all_tiles(b, 0, lambda cp: cp.start())
        o_ref[...] = jnp.zeros(o_ref.shape, F32)

    @pl.when(b + 1 < nb)
    def _():
        for_all_tiles(b + 1, (b + 1) % 2, lambda cp: cp.start())

    slot = b % 2
    for_all_tiles(b, slot, lambda cp: cp.wait())

    cols = []
    for h in range(n_heads):
        acc = jnp.zeros((head_dim, vbuf_ref.shape[-1]), F32)
        for t in range(MOBA_TOPK):
            blk = sel_ref[b, h * MOBA_TOPK + t]
            for pg in range(pages_per_block):
                k = h * per_head + t * pages_per_block + pg
                p_row = p_ref[0, blk * pages_per_block + pg, h:h + 1, :]
                acc = acc + vbuf_ref[slot, k] * p_row
        cols.append(jnp.sum(acc, axis=1, keepdims=True))
    col = jnp.concatenate(cols, axis=0)
    seq_i = lax.broadcasted_iota(jnp.int32, o_ref.shape, 1)
    o_ref[...] = jnp.where(seq_i == b, col, o_ref[...])

    @pl.when(b == nb - 1)
    def _():
        o_ref[...] = o_ref[...] + pself_e_ref[...] * v_t_ref[...]


def _moba_sample(q_t, v_t, sself, slopes, cache_k, cache_v, page_table, *, n_heads, head_dim):
    d_attn, n_seq = q_t.shape
    n_phys, page_size = cache_k.shape[0], cache_k.shape[1]
    n_pages = page_table.shape[1]
    past_len = n_pages * page_size
    pages_per_block = MOBA_BLOCK // page_size
    assert MOBA_BLOCK % page_size == 0 and past_len % MOBA_BLOCK == 0
    assert past_len // MOBA_BLOCK >= MOBA_TOPK
    ch = min(PAGES_PER_CHUNK, n_pages)
    assert n_pages % ch == 0
    ck = jnp.transpose(cache_k, (0, 2, 3, 1))
    cv = jnp.transpose(cache_v, (0, 2, 3, 1))
    qb = jnp.broadcast_to(q_t.T[:, :, None], (n_seq, d_attn, page_size))

    scores = pl.pallas_call(
        functools.partial(_sample_scores_kernel, ch=ch, n_heads=n_heads, head_dim=head_dim),
        grid_spec=pltpu.PrefetchScalarGridSpec(
            num_scalar_prefetch=1,
            grid=(n_seq, n_pages // ch),
            in_specs=[pl.BlockSpec((1, d_attn, page_size), lambda b, c, pt: (b, 0, 0)),
                      pl.BlockSpec(memory_space=pl.ANY)],
            out_specs=pl.BlockSpec((1, ch, n_heads, page_size), lambda b, c, pt: (b, c, 0, 0)),
            scratch_shapes=[pltpu.VMEM((2, ch, n_heads, head_dim, page_size), F32),
                            pltpu.SemaphoreType.DMA((2,))]),
        out_shape=jax.ShapeDtypeStruct((n_seq, n_pages, n_heads, page_size), F32),
        compiler_params=pltpu.CompilerParams(dimension_semantics=("arbitrary", "arbitrary"),
                                             vmem_limit_bytes=VMEM_LIMIT_BYTES),
        name="sample_scores",
    )(page_table, qb, ck)

    lanes = 128
    sself_b = jnp.broadcast_to(sself.T[:, :, None], (n_seq, n_heads, lanes))
    slopes_b = jnp.broadcast_to(slopes[:, None], (n_heads, lanes))
    probs, pself, sel = pl.pallas_call(
        functools.partial(_sample_probs_kernel, n_pages=n_pages, page_size=page_size,
                          past_len=past_len, head_dim=head_dim, n_heads=n_heads),
        grid=(n_seq,),
        in_specs=[pl.BlockSpec((1, n_pages, n_heads, page_size), lambda b: (b, 0, 0, 0)),
                  pl.BlockSpec((1, n_heads, lanes), lambda b: (b, 0, 0)),
                  pl.BlockSpec((n_heads, lanes), lambda b: (0, 0))],
        out_specs=[pl.BlockSpec((1, n_pages, n_heads, page_size), lambda b: (b, 0, 0, 0)),
                   pl.BlockSpec((1, n_heads, lanes), lambda b: (b, 0, 0)),
                   pl.BlockSpec((1, n_heads, lanes), lambda b: (b, 0, 0))],
        out_shape=[jax.ShapeDtypeStruct((n_seq, n_pages, n_heads, page_size), F32),
                   jax.ShapeDtypeStruct((n_seq, n_heads, lanes), F32),
                   jax.ShapeDtypeStruct((n_seq, n_heads, lanes), jnp.int32)],
        compiler_params=pltpu.CompilerParams(dimension_semantics=("arbitrary",)),
        name="sample_probs",
    )(scores, sself_b, slopes_b)

    sel_flat = sel[:, :, :MOBA_TOPK].reshape(n_seq, n_heads * MOBA_TOPK)
    pself_e = jnp.repeat(pself[:, :, 0].T, head_dim, axis=0)
    n_tiles = n_heads * MOBA_TOPK * pages_per_block
    return pl.pallas_call(
        functools.partial(_sample_pv_kernel, n_heads=n_heads, head_dim=head_dim,
                          pages_per_block=pages_per_block),
        grid_spec=pltpu.PrefetchScalarGridSpec(
            num_scalar_prefetch=2,
            grid=(n_seq,),
            in_specs=[pl.BlockSpec((1, n_pages, n_heads, page_size), lambda b, pt, sl: (b, 0, 0, 0)),
                      pl.BlockSpec((d_attn, n_seq), lambda b, pt, sl: (0, 0)),
                      pl.BlockSpec((d_attn, n_seq), lambda b, pt, sl: (0, 0)),
                      pl.BlockSpec(memory_space=pl.ANY)],
            out_specs=pl.BlockSpec((d_attn, n_seq), lambda b, pt, sl: (0, 0)),
            scratch_shapes=[pltpu.VMEM((2, n_tiles, head_dim, page_size), F32),
                            pltpu.SemaphoreType.DMA((2,))]),
        out_shape=jax.ShapeDtypeStruct((d_attn, n_seq), F32),
        compiler_params=pltpu.CompilerParams(dimension_semantics=("arbitrary",)),
        name="sample_pv",
    )(page_table, sel_flat, probs, pself_e, v_t, cv)


def _post_kernel(attn_ref, siga_ref, gconv_ref, x_ref, p_ref, wao_ref, wo_ref, n2_ref, wg_ref, wu_ref,
                 wd_ref, n3_ref, wple_ref, wpg_ref, y_ref, *, ff_chunks):
    y_attn = _dot(attn_ref[...], wao_ref[...])
    merged = siga_ref[...].astype(F32) * y_attn + gconv_ref[...].astype(F32)
    x1 = x_ref[...] + _dot(merged.astype(BF16), wo_ref[...])
    h2 = _rms_norm(x1, n2_ref[...]).astype(BF16)
    x2 = x1
    for lo, hi in ff_chunks:
        g = _dot(h2, wg_ref[:, lo:hi])
        u = _dot(h2, wu_ref[:, lo:hi])
        a = (g * jax.nn.sigmoid(g) * u).astype(BF16)
        x2 = x2 + _dot(a, wd_ref[lo:hi, :])
    h3 = _rms_norm(x2, n3_ref[...]).astype(BF16)
    ple = _dot(p_ref[...].astype(BF16), wple_ref[...])
    y_ref[...] = x2 + ple * jax.nn.sigmoid(_dot(h3, wpg_ref[...]))


def _post_call(attn, siga, gconv, x2d, p2d, weights, *, tm):
    (wao, wo, n2, wg, wu, wd, n3, wple, wpg) = weights
    rows, d_model = x2d.shape
    d_ff = wg.shape[1]
    chunk = 1024
    ff_chunks = tuple((lo, min(lo + chunk, d_ff)) for lo in range(0, d_ff, chunk))
    row = lambda w: pl.BlockSpec((tm, w), lambda i: (i, 0))
    return pl.pallas_call(
        functools.partial(_post_kernel, ff_chunks=ff_chunks),
        grid=(rows // tm,),
        in_specs=[row(attn.shape[1]), row(d_model), row(d_model), row(d_model), row(p2d.shape[1])]
                 + [_const_spec(w.shape) for w in weights],
        out_specs=row(d_model),
        out_shape=jax.ShapeDtypeStruct((rows, d_model), F32),
        compiler_params=pltpu.CompilerParams(dimension_semantics=("arbitrary",),
                                             vmem_limit_bytes=VMEM_LIMIT_BYTES),
        name="post",
    )(attn, siga, gconv, x2d, p2d, *weights)


def kernel(x_prompt, x_sample, p_prompt, p_sample, cache_k, cache_v, state_conv, page_table, norm1, w_in, q_norm, k_norm, conv_w, w_attn_out, w_conv_out, w_o, norm2, w_gate, w_up, w_down, norm3, w_ple, w_ple_gate):
    depth = w_in.shape[0]
    n_seq, seq_len, d_model = x_prompt.shape
    dec_seq, dec_len, _ = x_sample.shape
    head_dim = q_norm.shape[-1]
    d_attn = w_attn_out.shape[1]
    d_conv = w_conv_out.shape[1]
    n_heads = d_attn // head_dim
    assert dec_len == 1, "sample path handles one new token per sequence"
    assert conv_w.shape[1] == CONV_WIDTH
    assert seq_len % ROW_TILE == 0 and ROW_TILE % MOBA_BLOCK == 0
    assert n_heads % HEADS_PER_STEP == 0 and HEADS_PER_STEP * head_dim == 128
    slopes = 2.0 ** (-8.0 * jnp.arange(1, n_heads + 1, dtype=F32) / n_heads)

    yp = x_prompt
    ys = x_sample.reshape(dec_seq, d_model)
    outs = [[] for _ in range(6)]
    for i in range(depth):
        proj_w = (norm1[i][None, :],
                  w_in[i][:, :3 * d_attn].T.astype(BF16),
                  w_in[i][:, 3 * d_attn:].astype(BF16),
                  jnp.tile(q_norm[i], n_heads)[:, None],
                  jnp.tile(k_norm[i], n_heads)[:, None],
                  conv_w[i], w_conv_out[i].astype(BF16), n_heads, head_dim)
        post_w = (w_attn_out[i].astype(BF16), w_o[i].astype(BF16), norm2[i][None, :],
                  w_gate[i].astype(BF16), w_up[i].astype(BF16), w_down[i].astype(BF16),
                  norm3[i][None, :], w_ple[i].astype(BF16), w_ple_gate[i].astype(BF16))

        q_t, k_t, v_t, v_tb, k_rm, kbar, gconv, siga, tails = _proj_call(
            yp, None, proj_w, seq_mode=True, tm=ROW_TILE)
        n_blocks = seq_len // MOBA_BLOCK
        kbar_h = jnp.transpose(kbar.reshape(n_seq, n_blocks, n_heads, head_dim), (0, 2, 1, 3))
        attn = _moba_prompt_call(slopes, q_t, k_rm, v_tb, kbar_h, n_heads=n_heads, head_dim=head_dim)
        yp = _post_call(attn.reshape(n_seq * seq_len, d_attn), siga, gconv,
                        yp.reshape(n_seq * seq_len, d_model),
                        p_prompt[i].reshape(n_seq * seq_len, -1), post_w,
                        tm=ROW_TILE).reshape(n_seq, seq_len, d_model)
        to_cache_layout = lambda t, n, l: jnp.transpose(
            t.reshape(n, n_heads, head_dim, l), (0, 3, 1, 2))
        outs[0].append(to_cache_layout(k_t, n_seq, seq_len))
        outs[1].append(to_cache_layout(v_t, n_seq, seq_len))
        tiles_per_seq = seq_len // ROW_TILE
        outs[2].append(tails.reshape(n_seq, tiles_per_seq, CONV_WIDTH - 1, d_conv)[:, -1])

        hist = state_conv[i]
        sq_t, sk_t, sv_t, sself, sgconv, ssiga, su = _proj_call(
            ys[None], (hist[:, 0], hist[:, 1]), proj_w, seq_mode=False, tm=dec_seq)
        o_t = _moba_sample(sq_t[0], sv_t[0], sself, slopes, cache_k[i], cache_v[i], page_table,
                           n_heads=n_heads, head_dim=head_dim)
        ys = _post_call(o_t.T.astype(BF16), ssiga, sgconv, ys, p_sample[i].reshape(dec_seq, -1),
                        post_w, tm=dec_seq)
        outs[3].append(to_cache_layout(sk_t, 1, dec_seq).reshape(dec_seq, 1, n_heads, head_dim))
        outs[4].append(to_cache_layout(sv_t, 1, dec_seq).reshape(dec_seq, 1, n_heads, head_dim))
        outs[5].append(jnp.stack([hist[:, 1], su], axis=1))

    return (yp, ys.reshape(dec_seq, 1, d_model), *[jnp.stack(o) for o in outs])
```

```python
import functools

import jax
import jax.numpy as jnp
from jax import lax
from jax.experimental import pallas as pl
from jax.experimental.pallas import tpu as pltpu

RMS_EPS = 1e-6
NEG_INF = -1e30
BELOW_NEG_INF = -3e38
MOBA_BLOCK = 256
MOBA_BLOCK_LOG2 = 8
MOBA_TOPK = 3
CONV_WIDTH = 3
LOG2E = 1.4426950408889634
AUG_MAX_BLOCKS = 32
SPLIT = 3
AUG_KBLK = AUG_MAX_BLOCKS
AUG_KOFF = AUG_KBLK + SPLIT
AUG_ONE = AUG_KOFF + SPLIT
AUG_END = AUG_ONE + SPLIT
V_ROWS = 80
ATTN_HEADS_PER_STEP = 8
SUBLANES = 8
VMEM_LIMIT_BYTES = 60 * 1024 * 1024
ROW_TILE = 512
SELECT_TILE = 1024
PAGES_PER_CHUNK = 16

F32 = jnp.float32
BF16 = jnp.bfloat16


def _dot(a, b):
    return jnp.dot(a, b, preferred_element_type=F32)


def _dot_nt(a, b):
    return lax.dot_general(a, b, (((1,), (1,)), ((), ())), preferred_element_type=F32)


def _split_bf16(a):
    hi = a.astype(BF16)
    lo = (a - hi.astype(F32)).astype(BF16)
    return hi, lo


def _dot_3pass(a, b):
    ah, al = _split_bf16(a)
    bh, bl = _split_bf16(b)
    return _dot(ah, bh) + (_dot(ah, bl) + _dot(al, bh))


def _rms_norm(x, w):
    ms = jnp.mean(x * x, axis=-1, keepdims=True)
    return x * lax.rsqrt(ms + RMS_EPS) * w


def _head_norm_t(z_t, w_col, n_heads, head_dim):
    tokens = z_t.shape[1]
    z3 = z_t.reshape(n_heads, head_dim, tokens)
    ms = jnp.mean(z3 * z3, axis=1, keepdims=True)
    y = (z3 * lax.rsqrt(ms + RMS_EPS)).reshape(n_heads * head_dim, tokens)
    return y * w_col


def _const_spec(shape):
    zeros = (0,) * len(shape)
    return pl.BlockSpec(shape, lambda *_: zeros, pipeline_mode=pl.Buffered(1))


def _key_aug_rows(pos0, tokens, rows):
    r = lax.broadcasted_iota(jnp.int32, (rows, tokens), 0)
    pos = pos0 + lax.broadcasted_iota(jnp.int32, (rows, tokens), 1)
    block = jnp.right_shift(pos, MOBA_BLOCK_LOG2)
    offset = jnp.bitwise_and(pos, MOBA_BLOCK - 1).astype(F32)
    one = (r == block) | ((r >= AUG_ONE) & (r < AUG_END))
    is_blk = (r >= AUG_KBLK) & (r < AUG_KOFF)
    is_off = (r >= AUG_KOFF) & (r < AUG_ONE)
    return jnp.where(one, 1.0, jnp.where(is_blk, block.astype(F32), jnp.where(is_off, offset, 0.0)))


def _split3_bf16(a):
    p0 = a.astype(BF16).astype(F32)
    p1 = (a - p0).astype(BF16).astype(F32)
    p2 = (a - p0 - p1).astype(BF16).astype(F32)
    return p0, p1, p2


def _proj_kernel(*refs, seq_mode, tm, n_heads, head_dim, d_attn, d_conv, d_model):
    if seq_mode:
        (x_ref, n1_ref, wqkv_t_ref, wrest_ref, qn_ref, kn_ref, convw_ref, wco_ref,
         q_t_ref, k_t_ref, v_t_ref, v_tb_ref, kaug_ref, kbar_ref, gconv_ref, siga_ref, tail_ref,
         ubuf_ref) = refs
    else:
        (x_ref, h0_ref, h1_ref, n1_ref, wqkv_t_ref, wrest_ref, qn_ref, kn_ref, convw_ref, wco_ref,
         q_t_ref, k_t_ref, v_t_ref, sself_ref, gconv_ref, siga_ref, u_ref) = refs

    x = x_ref[0]
    h = _rms_norm(x, n1_ref[...]).astype(BF16)

    q_t = _head_norm_t(_dot_nt(wqkv_t_ref[0:d_attn, :], h), qn_ref[...], n_heads, head_dim)
    k_t = _head_norm_t(_dot_nt(wqkv_t_ref[d_attn:2 * d_attn, :], h), kn_ref[...], n_heads, head_dim)
    v_t = _dot_nt(wqkv_t_ref[2 * d_attn:3 * d_attn, :], h)
    q_t_ref[0] = q_t
    k_t_ref[0] = k_t
    v_t_ref[0] = v_t

    if seq_mode:
        aug_t = _key_aug_rows(pl.program_id(1) * tm, tm, head_dim)
        for hd in range(n_heads):
            k_h = jnp.concatenate([k_t[hd * head_dim:(hd + 1) * head_dim], aug_t], axis=0).T
            kaug_ref[0, hd] = k_h.astype(BF16)
            for i in range(tm // MOBA_BLOCK):
                kbar_ref[0, 0, hd, i:i + 1, :] = jnp.mean(
                    k_h[i * MOBA_BLOCK:(i + 1) * MOBA_BLOCK], axis=0, keepdims=True)
        pad_r = lax.broadcasted_iota(jnp.int32, (V_ROWS - head_dim, MOBA_BLOCK), 0)
        ones_pad = jnp.where(pad_r == 0, 1.0, 0.0)
        for i in range(tm // MOBA_BLOCK):
            lo = i * MOBA_BLOCK
            for hd in range(n_heads):
                v_h = v_t[hd * head_dim:(hd + 1) * head_dim, lo:lo + MOBA_BLOCK]
                v_tb_ref[0, i, hd * V_ROWS:(hd + 1) * V_ROWS, :] = jnp.concatenate(
                    [v_h, ones_pad], axis=0).astype(BF16)
    else:
        qk = (q_t * k_t).reshape(n_heads, head_dim, tm)
        sself_ref[...] = jnp.sum(qk, axis=1)

    zc = _dot(h, wrest_ref[:, 0:3 * d_conv])
    cb = zc[:, 0:d_conv]
    u = zc[:, d_conv:2 * d_conv] * zc[:, 2 * d_conv:3 * d_conv]
    if seq_mode:
        @pl.when(pl.program_id(1) == 0)
        def _():
            ubuf_ref[0:SUBLANES, :] = jnp.zeros((SUBLANES, d_conv), F32)
        ubuf_ref[SUBLANES:SUBLANES + tm, :] = u
        u_m2 = ubuf_ref[SUBLANES - 2:SUBLANES - 2 + tm, :]
        u_m1 = ubuf_ref[SUBLANES - 1:SUBLANES - 1 + tm, :]
        tail_ref[0] = u[tm - (CONV_WIDTH - 1):tm, :]
        ubuf_ref[0:SUBLANES, :] = u[tm - SUBLANES:tm, :]
    else:
        u_m2 = h0_ref[...]
        u_m1 = h1_ref[...]
        u_ref[...] = u
    conv = convw_ref[0:1, :] * u_m2 + convw_ref[1:2, :] * u_m1 + convw_ref[2:3, :] * u
    y_conv = _dot((cb * conv).astype(BF16), wco_ref[...])

    zg = _dot(h, wrest_ref[:, 3 * d_conv:3 * d_conv + 2 * d_model])
    siga_ref[...] = jax.nn.sigmoid(zg[:, 0:d_model]).astype(BF16)
    gconv_ref[...] = (jax.nn.sigmoid(zg[:, d_model:2 * d_model]) * y_conv).astype(BF16)


def _proj_call(x3, hist, weights, *, seq_mode, tm):
    (n1, wqkv_t, wrest, qn_col, kn_col, convw, wco, n_heads, head_dim) = weights
    n_seq, seq_len, d_model = x3.shape
    d_attn = n_heads * head_dim
    d_conv = convw.shape[1]
    n_tiles = seq_len // tm
    rows = n_seq * seq_len
    kern = functools.partial(_proj_kernel, seq_mode=seq_mode, tm=tm, n_heads=n_heads,
                             head_dim=head_dim, d_attn=d_attn, d_conv=d_conv, d_model=d_model)
    row_map = lambda b, t: (b * n_tiles + t, 0)
    w_specs = [_const_spec(n1.shape), _const_spec(wqkv_t.shape), _const_spec(wrest.shape),
               _const_spec(qn_col.shape), _const_spec(kn_col.shape), _const_spec(convw.shape),
               _const_spec(wco.shape)]
    t_spec = pl.BlockSpec((1, d_attn, tm), lambda b, t: (b, 0, t))
    t_shape = jax.ShapeDtypeStruct((n_seq, d_attn, seq_len), F32)
    gate_specs = [pl.BlockSpec((tm, d_model), row_map)] * 2
    gate_shapes = [jax.ShapeDtypeStruct((rows, d_model), BF16)] * 2
    x_spec = pl.BlockSpec((1, tm, d_model), lambda b, t: (b, t, 0))
    if seq_mode:
        nb_t = tm // MOBA_BLOCK
        in_specs = [x_spec] + w_specs
        args = (x3,)
        out_specs = [t_spec, t_spec, t_spec,
                     pl.BlockSpec((1, nb_t, n_heads * V_ROWS, MOBA_BLOCK), lambda b, t: (b, t, 0, 0)),
                     pl.BlockSpec((1, n_heads, tm, 2 * head_dim), lambda b, t: (b, 0, t, 0)),
                     pl.BlockSpec((1, 1, n_heads, nb_t, 2 * head_dim), lambda b, t: (b, t, 0, 0, 0)),
                     *gate_specs,
                     pl.BlockSpec((1, CONV_WIDTH - 1, d_conv), lambda b, t: (b * n_tiles + t, 0, 0))]
        out_shape = [t_shape, t_shape, t_shape,
                     jax.ShapeDtypeStruct((n_seq, seq_len // MOBA_BLOCK, n_heads * V_ROWS, MOBA_BLOCK),
                                          BF16),
                     jax.ShapeDtypeStruct((n_seq, n_heads, seq_len, 2 * head_dim), BF16),
                     jax.ShapeDtypeStruct((n_seq, n_tiles, n_heads, nb_t, 2 * head_dim), F32),
                     *gate_shapes,
                     jax.ShapeDtypeStruct((n_seq * n_tiles, CONV_WIDTH - 1, d_conv), F32)]
        scratch = [pltpu.VMEM((SUBLANES + tm, d_conv), F32)]
    else:
        h_spec = pl.BlockSpec((tm, d_conv), row_map)
        in_specs = [x_spec, h_spec, h_spec] + w_specs
        args = (x3, hist[0], hist[1])
        out_specs = [t_spec, t_spec, t_spec,
                     pl.BlockSpec((n_heads, tm), lambda b, t: (0, b * n_tiles + t)),
                     *gate_specs,
                     pl.BlockSpec((tm, d_conv), row_map)]
        out_shape = [t_shape, t_shape, t_shape,
                     jax.ShapeDtypeStruct((n_heads, rows), F32),
                     *gate_shapes,
                     jax.ShapeDtypeStruct((rows, d_conv), F32)]
        scratch = []
    return pl.pallas_call(
        kern, grid=(n_seq, n_tiles), in_specs=in_specs, out_specs=out_specs, out_shape=out_shape,
        scratch_shapes=scratch,
        compiler_params=pltpu.CompilerParams(dimension_semantics=("arbitrary", "arbitrary"),
                                             vmem_limit_bytes=VMEM_LIMIT_BYTES),
        name="proj_seq" if seq_mode else "proj_rows",
    )(*args, n1, wqkv_t, wrest, qn_col, kn_col, convw, wco)


def _topk_rows(gate, row_iota, k):
    n = gate.shape[0]
    picks = []
    for _ in range(k):
        mx = jnp.max(gate, axis=0, keepdims=True)
        idx = jnp.min(jnp.where(gate == mx, row_iota, n), axis=0, keepdims=True)
        picks.append(idx)
        gate = jnp.where(row_iota == idx, BELOW_NEG_INF, gate)
    return picks


def _select_kernel(slopes_ref, q_t_ref, kbar_ref, qaug_ref, *, head_dim, tq):
    hd = pl.program_id(1)
    pos0 = pl.program_id(2) * tq
    slope = slopes_ref[hd]
    q = q_t_ref[0]
    gate = _dot_3pass(kbar_ref[0, 0], q)
    blk_i = lax.broadcasted_iota(jnp.int32, (AUG_MAX_BLOCKS, tq), 0)
    pos = pos0 + lax.broadcasted_iota(jnp.int32, (AUG_MAX_BLOCKS, tq), 1)
    q_blk = jnp.right_shift(pos, MOBA_BLOCK_LOG2)
    past = blk_i < q_blk
    gate = jnp.where(past, gate, NEG_INF)
    mask = jnp.where(blk_i == q_blk, 0.0, NEG_INF)
    for idx in _topk_rows(gate, blk_i, MOBA_TOPK):
        mask = jnp.where((blk_i == idx) & past, 0.0, mask)
    tail_rows = head_dim - AUG_MAX_BLOCKS
    r = AUG_MAX_BLOCKS + lax.broadcasted_iota(jnp.int32, (tail_rows, tq), 0)
    q_pos = (pos0 + lax.broadcasted_iota(jnp.int32, (tail_rows, tq), 1)).astype(F32)
    c = jnp.full((tail_rows, tq), slope * LOG2E, F32)
    c_parts = _split3_bf16(c)
    v_parts = _split3_bf16(-(c * q_pos))
    tail = jnp.zeros((tail_rows, tq), F32)
    for i in range(SPLIT):
        tail = jnp.where(r == AUG_KBLK + i, MOBA_BLOCK * c_parts[i], tail)
        tail = jnp.where(r == AUG_KOFF + i, c_parts[i], tail)
        tail = jnp.where(r == AUG_ONE + i, v_parts[i], tail)
    scale = head_dim ** -0.5 * LOG2E
    qaug_ref[0, 0] = jnp.concatenate([q * scale, mask, tail], axis=0).astype(BF16)


def _select_call(slopes, q_t, kbar, *, n_heads, head_dim):
    n_seq, _, seq_len = q_t.shape
    tq = min(SELECT_TILE, seq_len)
    assert seq_len % tq == 0 and tq % MOBA_BLOCK == 0
    return pl.pallas_call(
        functools.partial(_select_kernel, head_dim=head_dim, tq=tq),
        grid=(n_seq, n_heads, seq_len // tq),
        in_specs=[pl.BlockSpec(memory_space=pltpu.SMEM),
                  pl.BlockSpec((1, head_dim, tq), lambda b, h, c: (b, h, c)),
                  pl.BlockSpec((1, 1, AUG_MAX_BLOCKS, head_dim), lambda b, h, c: (b, h, 0, 0))],
        out_specs=pl.BlockSpec((1, 1, 2 * head_dim, tq), lambda b, h, c: (b, h, 0, c)),
        out_shape=jax.ShapeDtypeStruct((n_seq, n_heads, 2 * head_dim, seq_len), BF16),
        compiler_params=pltpu.CompilerParams(
            dimension_semantics=("arbitrary", "arbitrary", "arbitrary")),
        name="moba_select",
    )(slopes, q_t, kbar)


def _moba_prompt_kernel(qaug_ref, kaug_ref, v_tb_ref, o_ref, s0_ref, s1_ref, m_ref, acc_ref,
                        *, head_dim, hps):
    j = pl.program_id(2)
    blk = MOBA_BLOCK
    key_i = lax.broadcasted_iota(jnp.int32, (blk, blk), 0)
    qry_i = lax.broadcasted_iota(jnp.int32, (blk, blk), 1)

    def scores(n, s_ref, own):
        for hh in range(hps):
            k_blk = kaug_ref[0, hh, pl.ds(pl.multiple_of(n * blk, blk), blk), :]
            s = _dot(k_blk, qaug_ref[0, hh])
            if own:
                s = jnp.where(key_i <= qry_i, s, NEG_INF)
            s_ref[hh] = s

    def accumulate(s_ref, n):
        for hh in range(hps):
            s = s_ref[hh]
            m = m_ref[hh]
            m_new = jnp.maximum(m, jnp.max(s, axis=0, keepdims=True))
            alpha = jnp.exp2(m - m_new)
            p = jnp.exp2(s - m_new).astype(BF16)
            pv = _dot(v_tb_ref[0, n, hh * V_ROWS:(hh + 1) * V_ROWS, :], p)
            acc_ref[hh] = alpha * acc_ref[hh] + pv
            m_ref[hh] = m_new

    m_ref[...] = jnp.full(m_ref.shape, BELOW_NEG_INF, F32)
    acc_ref[...] = jnp.zeros(acc_ref.shape, F32)
    scores(j, s0_ref, own=True)

    def pair(i, carry):
        scores(2 * i, s1_ref, own=False)
        accumulate(s0_ref, jnp.where(i == 0, j, 2 * i - 1))
        scores(2 * i + 1, s0_ref, own=False)
        accumulate(s1_ref, 2 * i)
        return carry

    lax.fori_loop(0, j // 2, pair, 0)

    @pl.when(j % 2 == 1)
    def _():
        scores(j - 1, s1_ref, own=False)
        accumulate(s0_ref, jnp.where(j == 1, j, j - 2))
        accumulate(s1_ref, j - 1)

    @pl.when(j % 2 == 0)
    def _():
        accumulate(s0_ref, jnp.where(j == 0, j, j - 1))

    o_t = jnp.concatenate(
        [acc_ref[hh, 0:head_dim, :] / acc_ref[hh, head_dim:head_dim + 1, :] for hh in range(hps)], axis=0)
    o_ref[0] = o_t.T.astype(o_ref.dtype)


def _moba_prompt_call(qaug, kaug, v_tb, *, n_heads, head_dim):
    n_seq, _, seq_len, hw = kaug.shape
    n_blocks = seq_len // MOBA_BLOCK
    blk = MOBA_BLOCK
    hps = ATTN_HEADS_PER_STEP
    assert n_heads % hps == 0 and (hps * head_dim) % 128 == 0
    resident = pl.Buffered(1)
    return pl.pallas_call(
        functools.partial(_moba_prompt_kernel, head_dim=head_dim, hps=hps),
        grid=(n_seq, n_heads // hps, n_blocks),
        in_specs=[
            pl.BlockSpec((1, hps, hw, blk), lambda b, hp, j: (b, hp, 0, j)),
            pl.BlockSpec((1, hps, seq_len, hw), lambda b, hp, j: (b, hp, 0, 0), pipeline_mode=resident),
            pl.BlockSpec((1, n_blocks, hps * V_ROWS, blk), lambda b, hp, j: (b, 0, hp, 0),
                         pipeline_mode=resident),
        ],
        out_specs=pl.BlockSpec((1, blk, hps * head_dim), lambda b, hp, j: (b, j, hp)),
        out_shape=jax.ShapeDtypeStruct((n_seq, seq_len, n_heads * head_dim), BF16),
        scratch_shapes=[pltpu.VMEM((hps, blk, blk), F32),
                        pltpu.VMEM((hps, blk, blk), F32),
                        pltpu.VMEM((hps, 1, blk), F32),
                        pltpu.VMEM((hps, V_ROWS, blk), F32)],
        compiler_params=pltpu.CompilerParams(
            dimension_semantics=("arbitrary", "arbitrary", "arbitrary"),
            vmem_limit_bytes=VMEM_LIMIT_BYTES),
        name="moba_prompt",
    )(qaug, kaug, v_tb)


def _sample_scores_kernel(pt_ref, qb_ref, ck_hbm, sc_ref, buf_ref, sem_ref, *, ch, n_heads, head_dim):
    b = pl.program_id(0)
    c = pl.program_id(1)
    nc = pl.num_programs(1)
    step = b * nc + c
    total = pl.num_programs(0) * nc

    def page_copy(seq, chunk, i, slot):
        page = pt_ref[seq, chunk * ch + i]
        return pltpu.make_async_copy(ck_hbm.at[page], buf_ref.at[slot, i], sem_ref.at[slot])

    def start(next_step):
        seq = next_step // nc
        chunk = next_step % nc
        for i in range(ch):
            page_copy(seq, chunk, i, next_step % 2).start()

    @pl.when(step == 0)
    def _():
        start(step)

    @pl.when(step + 1 < total)
    def _():
        start(step + 1)

    slot = step % 2
    for i in range(ch):
        page_copy(b, c, i, slot).wait()

    def body(i, carry):
        rows = []
        for h in range(n_heads):
            prod = buf_ref[slot, i, h] * qb_ref[0, h * head_dim:(h + 1) * head_dim, :]
            rows.append(jnp.sum(prod, axis=0, keepdims=True))
        sc_ref[0, i] = jnp.concatenate(rows, axis=0)
        return carry

    lax.fori_loop(0, ch, body, 0)


def _sample_probs_kernel(sc_ref, sself_ref, slopes_ref, p_ref, pself_ref, sel_ref,
                         *, n_pages, page_size, past_len, head_dim, n_heads):
    pages_per_block = MOBA_BLOCK // page_size
    n_blocks = n_pages // pages_per_block
    lanes = sself_ref.shape[-1]
    lane_i = lax.broadcasted_iota(jnp.int32, (n_heads, lanes), 1)
    gate = jnp.full((n_heads, lanes), NEG_INF, F32)
    for n in range(n_blocks):
        tot = sc_ref[0, n * pages_per_block]
        for pg in range(1, pages_per_block):
            tot = tot + sc_ref[0, n * pages_per_block + pg]
        g_n = jnp.sum(tot, axis=1, keepdims=True) * (1.0 / MOBA_BLOCK)
        gate = jnp.where(lane_i == n, g_n, gate)
    picks = []
    for _ in range(MOBA_TOPK):
        mx = jnp.max(gate, axis=1, keepdims=True)
        idx = jnp.min(jnp.where(gate == mx, lane_i, lanes), axis=1, keepdims=True)
        picks.append(idx)
        gate = jnp.where(lane_i == idx, BELOW_NEG_INF, gate)
    sel_vec = jnp.zeros((n_heads, lanes), jnp.int32)
    for t, idx in enumerate(picks):
        sel_vec = jnp.where(lane_i == t, idx, sel_vec)
    sel_ref[0] = sel_vec

    scale = head_dim ** -0.5
    slope = slopes_ref[...]
    pos_in_page = lax.broadcasted_iota(jnp.int32, (n_heads, page_size), 1)
    s_self = sself_ref[0] * scale

    def page_scores(pg):
        selected = picks[0] == (pg // pages_per_block)
        for idx in picks[1:]:
            selected = selected | (idx == (pg // pages_per_block))
        dist = (past_len - pg * page_size - pos_in_page).astype(F32)
        s = sc_ref[0, pg] * scale - slope[:, 0:page_size] * dist
        return jnp.where(selected, s, NEG_INF)

    m = s_self
    for pg in range(n_pages):
        m = jnp.maximum(m, jnp.max(page_scores(pg), axis=1, keepdims=True))
    e_self = jnp.exp(s_self - m)
    l = e_self
    for pg in range(n_pages):
        e = jnp.exp(page_scores(pg) - m[:, 0:page_size])
        p_ref[0, pg] = e
        l = l + jnp.sum(e, axis=1, keepdims=True)
    inv = 1.0 / l
    for pg in range(n_pages):
        p_ref[0, pg] = p_ref[0, pg] * inv[:, 0:page_size]
    pself_ref[0] = e_self * inv


def _sample_pv_kernel(pt_ref, sel_ref, p_ref, pself_e_ref, v_t_ref, cv_hbm, o_ref, vbuf_ref, sem_ref,
                      *, n_heads, head_dim, pages_per_block):
    b = pl.program_id(0)
    nb = pl.num_programs(0)
    per_head = MOBA_TOPK * pages_per_block

    def tile_copy(seq, h, t, pg, slot):
        blk = sel_ref[seq, h * MOBA_TOPK + t]
        page = pt_ref[seq, blk * pages_per_block + pg]
        k = h * per_head + t * pages_per_block + pg
        return pltpu.make_async_copy(cv_hbm.at[page, h], vbuf_ref.at[slot, k], sem_ref.at[slot])

    def for_all_tiles(seq, slot, fn):
        for h in range(n_heads):
            for t in range(MOBA_TOPK):
                for pg in range(pages_per_block):
                    fn(tile_copy(seq, h, t, pg, slot))

    @pl.when(b == 0)
    def _():
        for_all_tiles(b, 0, lambda cp: cp.start())
        o_ref[...] = jnp.zeros(o_ref.shape, F32)

    @pl.when(b + 1 < nb)
    def _():
        for_all_tiles(b + 1, (b + 1) % 2, lambda cp: cp.start())

    slot = b % 2
    for_all_tiles(b, slot, lambda cp: cp.wait())

    cols = []
    for h in range(n_heads):
        acc = jnp.zeros((head_dim, vbuf_ref.shape[-1]), F32)
        for t in range(MOBA_TOPK):
            blk = sel_ref[b, h * MOBA_TOPK + t]
            for pg in range(pages_per_block):
                k = h * per_head + t * pages_per_block + pg
                p_row = p_ref[0, blk * pages_per_block + pg, h:h + 1, :]
                acc = acc + vbuf_ref[slot, k] * p_row
        cols.append(jnp.sum(acc, axis=1, keepdims=True))
    col = jnp.concatenate(cols, axis=0)
    seq_i = lax.broadcasted_iota(jnp.int32, o_ref.shape, 1)
    o_ref[...] = jnp.where(seq_i == b, col, o_ref[...])

    @pl.when(b == nb - 1)
    def _():
        o_ref[...] = o_ref[...] + pself_e_ref[...] * v_t_ref[...]


def _moba_sample(q_t, v_t, sself, slopes, cache_k, cache_v, page_table, *, n_heads, head_dim):
    d_attn, n_seq = q_t.shape
    n_phys, page_size = cache_k.shape[0], cache_k.shape[1]
    n_pages = page_table.shape[1]
    past_len = n_pages * page_size
    pages_per_block = MOBA_BLOCK // page_size
    assert MOBA_BLOCK % page_size == 0 and past_len % MOBA_BLOCK == 0
    assert past_len // MOBA_BLOCK >= MOBA_TOPK
    ch = min(PAGES_PER_CHUNK, n_pages)
    assert n_pages % ch == 0
    ck = jnp.transpose(cache_k, (0, 2, 3, 1))
    cv = jnp.transpose(cache_v, (0, 2, 3, 1))
    qb = jnp.broadcast_to(q_t.T[:, :, None], (n_seq, d_attn, page_size))

    scores = pl.pallas_call(
        functools.partial(_sample_scores_kernel, ch=ch, n_heads=n_heads, head_dim=head_dim),
        grid_spec=pltpu.PrefetchScalarGridSpec(
            num_scalar_prefetch=1,
            grid=(n_seq, n_pages // ch),
            in_specs=[pl.BlockSpec((1, d_attn, page_size), lambda b, c, pt: (b, 0, 0)),
                      pl.BlockSpec(memory_space=pl.ANY)],
            out_specs=pl.BlockSpec((1, ch, n_heads, page_size), lambda b, c, pt: (b, c, 0, 0)),
            scratch_shapes=[pltpu.VMEM((2, ch, n_heads, head_dim, page_size), F32),
                            pltpu.SemaphoreType.DMA((2,))]),
        out_shape=jax.ShapeDtypeStruct((n_seq, n_pages, n_heads, page_size), F32),
        compiler_params=pltpu.CompilerParams(dimension_semantics=("arbitrary", "arbitrary"),
                                             vmem_limit_bytes=VMEM_LIMIT_BYTES),
        name="sample_scores",
    )(page_table, qb, ck)

    lanes = 128
    sself_b = jnp.broadcast_to(sself.T[:, :, None], (n_seq, n_heads, lanes))
    slopes_b = jnp.broadcast_to(slopes[:, None], (n_heads, lanes))
    probs, pself, sel = pl.pallas_call(
        functools.partial(_sample_probs_kernel, n_pages=n_pages, page_size=page_size,
                          past_len=past_len, head_dim=head_dim, n_heads=n_heads),
        grid=(n_seq,),
        in_specs=[pl.BlockSpec((1, n_pages, n_heads, page_size), lambda b: (b, 0, 0, 0)),
                  pl.BlockSpec((1, n_heads, lanes), lambda b: (b, 0, 0)),
                  pl.BlockSpec((n_heads, lanes), lambda b: (0, 0))],
        out_specs=[pl.BlockSpec((1, n_pages, n_heads, page_size), lambda b: (b, 0, 0, 0)),
                   pl.BlockSpec((1, n_heads, lanes), lambda b: (b, 0, 0)),
                   pl.BlockSpec((1, n_heads, lanes), lambda b: (b, 0, 0))],
        out_shape=[jax.ShapeDtypeStruct((n_seq, n_pages, n_heads, page_size), F32),
                   jax.ShapeDtypeStruct((n_seq, n_heads, lanes), F32),
                   jax.ShapeDtypeStruct((n_seq, n_heads, lanes), jnp.int32)],
        compiler_params=pltpu.CompilerParams(dimension_semantics=("arbitrary",)),
        name="sample_probs",
    )(scores, sself_b, slopes_b)

    sel_flat = sel[:, :, :MOBA_TOPK].reshape(n_seq, n_heads * MOBA_TOPK)
    pself_e = jnp.repeat(pself[:, :, 0].T, head_dim, axis=0)
    n_tiles = n_heads * MOBA_TOPK * pages_per_block
    return pl.pallas_call(
        functools.partial(_sample_pv_kernel, n_heads=n_heads, head_dim=head_dim,
                          pages_per_block=pages_per_block),
        grid_spec=pltpu.PrefetchScalarGridSpec(
            num_scalar_prefetch=2,
            grid=(n_seq,),
            in_specs=[pl.BlockSpec((1, n_pages, n_heads, page_size), lambda b, pt, sl: (b, 0, 0, 0)),
                      pl.BlockSpec((d_attn, n_seq), lambda b, pt, sl: (0, 0)),
                      pl.BlockSpec((d_attn, n_seq), lambda b, pt, sl: (0, 0)),
                      pl.BlockSpec(memory_space=pl.ANY)],
            out_specs=pl.BlockSpec((d_attn, n_seq), lambda b, pt, sl: (0, 0)),
            scratch_shapes=[pltpu.VMEM((2, n_tiles, head_dim, page_size), F32),
                            pltpu.SemaphoreType.DMA((2,))]),
        out_shape=jax.ShapeDtypeStruct((d_attn, n_seq), F32),
        compiler_params=pltpu.CompilerParams(dimension_semantics=("arbitrary",)),
        name="sample_pv",
    )(page_table, sel_flat, probs, pself_e, v_t, cv)


def _post_kernel(attn_ref, siga_ref, gconv_ref, x_ref, p_ref, wao_ref, wo_ref, n2_ref, wg_ref, wu_ref,
                 wd_ref, n3_ref, wple_ref, wpg_ref, y_ref, *, ff_chunks):
    y_attn = _dot(attn_ref[...], wao_ref[...])
    merged = siga_ref[...].astype(F32) * y_attn + gconv_ref[...].astype(F32)
    x1 = x_ref[...] + _dot(merged.astype(BF16), wo_ref[...])
    h2 = _rms_norm(x1, n2_ref[...]).astype(BF16)
    x2 = x1
    for lo, hi in ff_chunks:
        g = _dot(h2, wg_ref[:, lo:hi])
        u = _dot(h2, wu_ref[:, lo:hi])
        a = (g * jax.nn.sigmoid(g) * u).astype(BF16)
        x2 = x2 + _dot(a, wd_ref[lo:hi, :])
    h3 = _rms_norm(x2, n3_ref[...]).astype(BF16)
    ple = _dot(p_ref[...].astype(BF16), wple_ref[...])
    y_ref[...] = x2 + ple * jax.nn.sigmoid(_dot(h3, wpg_ref[...]))


def _post_call(attn, siga, gconv, x2d, p2d, weights, *, tm):
    (wao, wo, n2, wg, wu, wd, n3, wple, wpg) = weights
    rows, d_model = x2d.shape
    d_ff = wg.shape[1]
    chunk = 1024
    ff_chunks = tuple((lo, min(lo + chunk, d_ff)) for lo in range(0, d_ff, chunk))
    row = lambda w: pl.BlockSpec((tm, w), lambda i: (i, 0))
    return pl.pallas_call(
        functools.partial(_post_kernel, ff_chunks=ff_chunks),
        grid=(rows // tm,),
        in_specs=[row(attn.shape[1]), row(d_model), row(d_model), row(d_model), row(p2d.shape[1])]
                 + [_const_spec(w.shape) for w in weights],
        out_specs=row(d_model),
        out_shape=jax.ShapeDtypeStruct((rows, d_model), F32),
        compiler_params=pltpu.CompilerParams(dimension_semantics=("arbitrary",),
                                             vmem_limit_bytes=VMEM_LIMIT_BYTES),
        name="post",
    )(attn, siga, gconv, x2d, p2d, *weights)


def kernel(x_prompt, x_sample, p_prompt, p_sample, cache_k, cache_v, state_conv, page_table, norm1, w_in, q_norm, k_norm, conv_w, w_attn_out, w_conv_out, w_o, norm2, w_gate, w_up, w_down, norm3, w_ple, w_ple_gate):
    depth = w_in.shape[0]
    n_seq, seq_len, d_model = x_prompt.shape
    dec_seq, dec_len, _ = x_sample.shape
    head_dim = q_norm.shape[-1]
    d_attn = w_attn_out.shape[1]
    d_conv = w_conv_out.shape[1]
    n_heads = d_attn // head_dim
    n_blocks = seq_len // MOBA_BLOCK
    assert dec_len == 1, "sample path handles one new token per sequence"
    assert conv_w.shape[1] == CONV_WIDTH
    assert seq_len % ROW_TILE == 0 and ROW_TILE % MOBA_BLOCK == 0
    assert MOBA_BLOCK == 1 << MOBA_BLOCK_LOG2
    assert n_blocks <= AUG_MAX_BLOCKS and AUG_END <= head_dim and head_dim < V_ROWS
    slopes = 2.0 ** (-8.0 * jnp.arange(1, n_heads + 1, dtype=F32) / n_heads)

    yp = x_prompt
    ys = x_sample.reshape(dec_seq, d_model)
    outs = [[] for _ in range(6)]
    for i in range(depth):
        proj_w = (norm1[i][None, :],
                  w_in[i][:, :3 * d_attn].T.astype(BF16),
                  w_in[i][:, 3 * d_attn:].astype(BF16),
                  jnp.tile(q_norm[i], n_heads)[:, None],
                  jnp.tile(k_norm[i], n_heads)[:, None],
                  conv_w[i], w_conv_out[i].astype(BF16), n_heads, head_dim)
        post_w = (w_attn_out[i].astype(BF16), w_o[i].astype(BF16), norm2[i][None, :],
                  w_gate[i].astype(BF16), w_up[i].astype(BF16), w_down[i].astype(BF16),
                  norm3[i][None, :], w_ple[i].astype(BF16), w_ple_gate[i].astype(BF16))

        q_t, k_t, v_t, v_tb, kaug, kbar, gconv, siga, tails = _proj_call(
            yp, None, proj_w, seq_mode=True, tm=ROW_TILE)
        kbar_h = jnp.transpose(kbar[..., :head_dim], (0, 2, 1, 3, 4)).reshape(
            n_seq, n_heads, n_blocks, head_dim)
        kbar_h = jnp.pad(kbar_h, ((0, 0), (0, 0), (0, AUG_MAX_BLOCKS - n_blocks), (0, 0)))
        qaug = _select_call(slopes, q_t, kbar_h, n_heads=n_heads, head_dim=head_dim)
        attn = _moba_prompt_call(qaug, kaug, v_tb, n_heads=n_heads, head_dim=head_dim)
        yp = _post_call(attn.reshape(n_seq * seq_len, d_attn), siga, gconv,
                        yp.reshape(n_seq * seq_len, d_model),
                        p_prompt[i].reshape(n_seq * seq_len, -1), post_w,
                        tm=ROW_TILE).reshape(n_seq, seq_len, d_model)
        to_cache_layout = lambda t, n, l: jnp.transpose(
            t.reshape(n, n_heads, head_dim, l), (0, 3, 1, 2))
        outs[0].append(to_cache_layout(k_t, n_seq, seq_len))
        outs[1].append(to_cache_layout(v_t, n_seq, seq_len))
        tiles_per_seq = seq_len // ROW_TILE
        outs[2].append(tails.reshape(n_seq, tiles_per_seq, CONV_WIDTH - 1, d_conv)[:, -1])

        hist = state_conv[i]
        sq_t, sk_t, sv_t, sself, sgconv, ssiga, su = _proj_call(
            ys[None], (hist[:, 0], hist[:, 1]), proj_w, seq_mode=False, tm=dec_seq)
        o_t = _moba_sample(sq_t[0], sv_t[0], sself, slopes, cache_k[i], cache_v[i], page_table,
                           n_heads=n_heads, head_dim=head_dim)
        ys = _post_call(o_t.T.astype(BF16), ssiga, sgconv, ys, p_sample[i].reshape(dec_seq, -1),
                        post_w, tm=dec_seq)
        outs[3].append(to_cache_layout(sk_t, 1, dec_seq).reshape(dec_seq, 1, n_heads, head_dim))
        outs[4].append(to_cache_layout(sv_t, 1, dec_seq).reshape(dec_seq, 1, n_heads, head_dim))
        outs[5].append(jnp.stack([hist[:, 1], su], axis=1))

    return (yp, ys.reshape(dec_seq, 1, d_model), *[jnp.stack(o) for o in outs])
```

```python
import functools

import jax
import jax.numpy as jnp
from jax import lax
from jax.experimental import pallas as pl
from jax.experimental.pallas import tpu as pltpu

RMS_EPS = 1e-6
NEG_INF = -1e30
BELOW_NEG_INF = -3e38
MOBA_BLOCK = 256
MOBA_BLOCK_LOG2 = 8
MOBA_TOPK = 3
CONV_WIDTH = 3
LOG2E = 1.4426950408889634
AUG_MAX_BLOCKS = 32
SPLIT = 3
AUG_KBLK = AUG_MAX_BLOCKS
AUG_KOFF = AUG_KBLK + SPLIT
AUG_ONE = AUG_KOFF + SPLIT
AUG_END = AUG_ONE + SPLIT
V_ROWS = 80
ATTN_HEADS_PER_STEP = 8
ATTN_BLOCKS_PER_ITER = 4
SUBLANES = 8
VMEM_LIMIT_BYTES = 60 * 1024 * 1024
ROW_TILE = 512
SELECT_TILE = 2048
SAMPLE_SEQS_PER_STEP = 4
PAGES_PER_CHUNK = 32

F32 = jnp.float32
BF16 = jnp.bfloat16


def _dot(a, b):
    return jnp.dot(a, b, preferred_element_type=F32)


def _dot_nt(a, b):
    return lax.dot_general(a, b, (((1,), (1,)), ((), ())), preferred_element_type=F32)


def _split_bf16(a):
    hi = a.astype(BF16)
    lo = (a - hi.astype(F32)).astype(BF16)
    return hi, lo


def _dot_3pass(a, b):
    ah, al = _split_bf16(a)
    bh, bl = _split_bf16(b)
    return _dot(ah, bh) + (_dot(ah, bl) + _dot(al, bh))


def _rms_norm(x, w):
    ms = jnp.mean(x * x, axis=-1, keepdims=True)
    return x * lax.rsqrt(ms + RMS_EPS) * w


def _head_norm_t(z_t, w_col, n_heads, head_dim):
    tokens = z_t.shape[1]
    z3 = z_t.reshape(n_heads, head_dim, tokens)
    ms = jnp.mean(z3 * z3, axis=1, keepdims=True)
    y = (z3 * lax.rsqrt(ms + RMS_EPS)).reshape(n_heads * head_dim, tokens)
    return y * w_col


def _const_spec(shape):
    zeros = (0,) * len(shape)
    return pl.BlockSpec(shape, lambda *_: zeros, pipeline_mode=pl.Buffered(1))


def _key_aug_rows(pos0, tokens, rows):
    r = lax.broadcasted_iota(jnp.int32, (rows, tokens), 0)
    pos = pos0 + lax.broadcasted_iota(jnp.int32, (rows, tokens), 1)
    block = jnp.right_shift(pos, MOBA_BLOCK_LOG2)
    offset = jnp.bitwise_and(pos, MOBA_BLOCK - 1).astype(F32)
    one = (r == block) | ((r >= AUG_ONE) & (r < AUG_END))
    is_blk = (r >= AUG_KBLK) & (r < AUG_KOFF)
    is_off = (r >= AUG_KOFF) & (r < AUG_ONE)
    return jnp.where(one, 1.0, jnp.where(is_blk, block.astype(F32), jnp.where(is_off, offset, 0.0)))


def _split3_bf16(a):
    p0 = a.astype(BF16).astype(F32)
    p1 = (a - p0).astype(BF16).astype(F32)
    p2 = (a - p0 - p1).astype(BF16).astype(F32)
    return p0, p1, p2


def _proj_kernel(*refs, seq_mode, tm, n_heads, head_dim, d_attn, d_conv, d_model):
    if seq_mode:
        (x_ref, n1_ref, wqkv_t_ref, wrest_ref, qn_ref, kn_ref, convw_ref, wco_ref,
         q_t_ref, k_t_ref, v_t_ref, v_tb_ref, kaug_ref, kbar_ref, gconv_ref, siga_ref, tail_ref,
         ubuf_ref) = refs
    else:
        (x_ref, h0_ref, h1_ref, n1_ref, wqkv_t_ref, wrest_ref, qn_ref, kn_ref, convw_ref, wco_ref,
         q_t_ref, k_t_ref, v_t_ref, sself_ref, gconv_ref, siga_ref, u_ref) = refs

    x = x_ref[0]
    h = _rms_norm(x, n1_ref[...]).astype(BF16)

    q_t = _head_norm_t(_dot_nt(wqkv_t_ref[0:d_attn, :], h), qn_ref[...], n_heads, head_dim)
    k_t = _head_norm_t(_dot_nt(wqkv_t_ref[d_attn:2 * d_attn, :], h), kn_ref[...], n_heads, head_dim)
    v_t = _dot_nt(wqkv_t_ref[2 * d_attn:3 * d_attn, :], h)
    q_t_ref[0] = q_t
    k_t_ref[0] = k_t
    v_t_ref[0] = v_t

    if seq_mode:
        aug_t = _key_aug_rows(pl.program_id(1) * tm, tm, head_dim)
        for hd in range(n_heads):
            k_h = jnp.concatenate([k_t[hd * head_dim:(hd + 1) * head_dim], aug_t], axis=0).T
            kaug_ref[0, hd] = k_h.astype(BF16)
            for i in range(tm // MOBA_BLOCK):
                kbar_ref[0, 0, hd, i:i + 1, :] = jnp.mean(
                    k_h[i * MOBA_BLOCK:(i + 1) * MOBA_BLOCK], axis=0, keepdims=True)
        pad_r = lax.broadcasted_iota(jnp.int32, (V_ROWS - head_dim, MOBA_BLOCK), 0)
        ones_pad = jnp.where(pad_r == 0, 1.0, 0.0)
        for i in range(tm // MOBA_BLOCK):
            lo = i * MOBA_BLOCK
            for hd in range(n_heads):
                v_h = v_t[hd * head_dim:(hd + 1) * head_dim, lo:lo + MOBA_BLOCK]
                v_tb_ref[0, i, hd * V_ROWS:(hd + 1) * V_ROWS, :] = jnp.concatenate(
                    [v_h, ones_pad], axis=0).astype(BF16)
    else:
        qk = (q_t * k_t).reshape(n_heads, head_dim, tm)
        sself_ref[...] = jnp.sum(qk, axis=1)

    zc = _dot(h, wrest_ref[:, 0:3 * d_conv])
    cb = zc[:, 0:d_conv]
    u = zc[:, d_conv:2 * d_conv] * zc[:, 2 * d_conv:3 * d_conv]
    if seq_mode:
        @pl.when(pl.program_id(1) == 0)
        def _():
            ubuf_ref[0:SUBLANES, :] = jnp.zeros((SUBLANES, d_conv), F32)
        ubuf_ref[SUBLANES:SUBLANES + tm, :] = u
        u_m2 = ubuf_ref[SUBLANES - 2:SUBLANES - 2 + tm, :]
        u_m1 = ubuf_ref[SUBLANES - 1:SUBLANES - 1 + tm, :]
        tail_ref[0] = u[tm - (CONV_WIDTH - 1):tm, :]
        ubuf_ref[0:SUBLANES, :] = u[tm - SUBLANES:tm, :]
    else:
        u_m2 = h0_ref[...]
        u_m1 = h1_ref[...]
        u_ref[...] = u
    conv = convw_ref[0:1, :] * u_m2 + convw_ref[1:2, :] * u_m1 + convw_ref[2:3, :] * u
    y_conv = _dot((cb * conv).astype(BF16), wco_ref[...])

    zg = _dot(h, wrest_ref[:, 3 * d_conv:3 * d_conv + 2 * d_model])
    siga_ref[...] = jax.nn.sigmoid(zg[:, 0:d_model]).astype(BF16)
    gconv_ref[...] = (jax.nn.sigmoid(zg[:, d_model:2 * d_model]) * y_conv).astype(BF16)


def _proj_call(x3, hist, weights, *, seq_mode, tm):
    (n1, wqkv_t, wrest, qn_col, kn_col, convw, wco, n_heads, head_dim) = weights
    n_seq, seq_len, d_model = x3.shape
    d_attn = n_heads * head_dim
    d_conv = convw.shape[1]
    n_tiles = seq_len // tm
    rows = n_seq * seq_len
    kern = functools.partial(_proj_kernel, seq_mode=seq_mode, tm=tm, n_heads=n_heads,
                             head_dim=head_dim, d_attn=d_attn, d_conv=d_conv, d_model=d_model)
    row_map = lambda b, t: (b * n_tiles + t, 0)
    w_specs = [_const_spec(n1.shape), _const_spec(wqkv_t.shape), _const_spec(wrest.shape),
               _const_spec(qn_col.shape), _const_spec(kn_col.shape), _const_spec(convw.shape),
               _const_spec(wco.shape)]
    t_spec = pl.BlockSpec((1, d_attn, tm), lambda b, t: (b, 0, t))
    t_shape = jax.ShapeDtypeStruct((n_seq, d_attn, seq_len), F32)
    gate_specs = [pl.BlockSpec((tm, d_model), row_map)] * 2
    gate_shapes = [jax.ShapeDtypeStruct((rows, d_model), BF16)] * 2
    x_spec = pl.BlockSpec((1, tm, d_model), lambda b, t: (b, t, 0))
    if seq_mode:
        nb_t = tm // MOBA_BLOCK
        in_specs = [x_spec] + w_specs
        args = (x3,)
        out_specs = [t_spec, t_spec, t_spec,
                     pl.BlockSpec((1, nb_t, n_heads * V_ROWS, MOBA_BLOCK), lambda b, t: (b, t, 0, 0)),
                     pl.BlockSpec((1, n_heads, tm, 2 * head_dim), lambda b, t: (b, 0, t, 0)),
                     pl.BlockSpec((1, 1, n_heads, nb_t, 2 * head_dim), lambda b, t: (b, t, 0, 0, 0)),
                     *gate_specs,
                     pl.BlockSpec((1, CONV_WIDTH - 1, d_conv), lambda b, t: (b * n_tiles + t, 0, 0))]
        out_shape = [t_shape, t_shape, t_shape,
                     jax.ShapeDtypeStruct((n_seq, seq_len // MOBA_BLOCK, n_heads * V_ROWS, MOBA_BLOCK),
                                          BF16),
                     jax.ShapeDtypeStruct((n_seq, n_heads, seq_len, 2 * head_dim), BF16),
                     jax.ShapeDtypeStruct((n_seq, n_tiles, n_heads, nb_t, 2 * head_dim), F32),
                     *gate_shapes,
                     jax.ShapeDtypeStruct((n_seq * n_tiles, CONV_WIDTH - 1, d_conv), F32)]
        scratch = [pltpu.VMEM((SUBLANES + tm, d_conv), F32)]
    else:
        h_spec = pl.BlockSpec((tm, d_conv), row_map)
        in_specs = [x_spec, h_spec, h_spec] + w_specs
        args = (x3, hist[0], hist[1])
        out_specs = [t_spec, t_spec, t_spec,
                     pl.BlockSpec((n_heads, tm), lambda b, t: (0, b * n_tiles + t)),
                     *gate_specs,
                     pl.BlockSpec((tm, d_conv), row_map)]
        out_shape = [t_shape, t_shape, t_shape,
                     jax.ShapeDtypeStruct((n_heads, rows), F32),
                     *gate_shapes,
                     jax.ShapeDtypeStruct((rows, d_conv), F32)]
        scratch = []
    return pl.pallas_call(
        kern, grid=(n_seq, n_tiles), in_specs=in_specs, out_specs=out_specs, out_shape=out_shape,
        scratch_shapes=scratch,
        compiler_params=pltpu.CompilerParams(dimension_semantics=("arbitrary", "arbitrary"),
                                             vmem_limit_bytes=VMEM_LIMIT_BYTES),
        name="proj_seq" if seq_mode else "proj_rows",
    )(*args, n1, wqkv_t, wrest, qn_col, kn_col, convw, wco)


def _topk_rows(gate, row_iota, k):
    n = gate.shape[0]
    picks = []
    for _ in range(k):
        mx = jnp.max(gate, axis=0, keepdims=True)
        idx = jnp.min(jnp.where(gate == mx, row_iota, n), axis=0, keepdims=True)
        picks.append(idx)
        gate = jnp.where(row_iota == idx, BELOW_NEG_INF, gate)
    return picks


def _select_kernel(slopes_ref, q_t_ref, kbar_ref, qaug_ref, *, head_dim, tq):
    hd = pl.program_id(1)
    pos0 = pl.program_id(2) * tq
    slope = slopes_ref[hd]
    q = q_t_ref[0]
    gate = _dot_3pass(kbar_ref[0, 0], q)
    blk_i = lax.broadcasted_iota(jnp.int32, (AUG_MAX_BLOCKS, tq), 0)
    pos = pos0 + lax.broadcasted_iota(jnp.int32, (AUG_MAX_BLOCKS, tq), 1)
    q_blk = jnp.right_shift(pos, MOBA_BLOCK_LOG2)
    past = blk_i < q_blk
    gate = jnp.where(past, gate, NEG_INF)
    mask = jnp.where(blk_i == q_blk, 0.0, NEG_INF)
    for idx in _topk_rows(gate, blk_i, MOBA_TOPK):
        mask = jnp.where((blk_i == idx) & past, 0.0, mask)
    tail_rows = head_dim - AUG_MAX_BLOCKS
    r = AUG_MAX_BLOCKS + lax.broadcasted_iota(jnp.int32, (tail_rows, tq), 0)
    q_pos = (pos0 + lax.broadcasted_iota(jnp.int32, (tail_rows, tq), 1)).astype(F32)
    c = jnp.full((tail_rows, tq), slope * LOG2E, F32)
    c_parts = _split3_bf16(c)
    v_parts = _split3_bf16(-(c * q_pos))
    tail = jnp.zeros((tail_rows, tq), F32)
    for i in range(SPLIT):
        tail = jnp.where(r == AUG_KBLK + i, MOBA_BLOCK * c_parts[i], tail)
        tail = jnp.where(r == AUG_KOFF + i, c_parts[i], tail)
        tail = jnp.where(r == AUG_ONE + i, v_parts[i], tail)
    scale = head_dim ** -0.5 * LOG2E
    qaug_ref[0, 0] = jnp.concatenate([q * scale, mask, tail], axis=0).astype(BF16)


def _select_call(slopes, q_t, kbar, *, n_heads, head_dim):
    n_seq, _, seq_len = q_t.shape
    tq = min(SELECT_TILE, seq_len)
    assert seq_len % tq == 0 and tq % MOBA_BLOCK == 0
    return pl.pallas_call(
        functools.partial(_select_kernel, head_dim=head_dim, tq=tq),
        grid=(n_seq, n_heads, seq_len // tq),
        in_specs=[pl.BlockSpec(memory_space=pltpu.SMEM),
                  pl.BlockSpec((1, head_dim, tq), lambda b, h, c: (b, h, c)),
                  pl.BlockSpec((1, 1, AUG_MAX_BLOCKS, head_dim), lambda b, h, c: (b, h, 0, 0))],
        out_specs=pl.BlockSpec((1, 1, 2 * head_dim, tq), lambda b, h, c: (b, h, 0, c)),
        out_shape=jax.ShapeDtypeStruct((n_seq, n_heads, 2 * head_dim, seq_len), BF16),
        compiler_params=pltpu.CompilerParams(
            dimension_semantics=("arbitrary", "arbitrary", "arbitrary")),
        name="moba_select",
    )(slopes, q_t, kbar)


def _moba_prompt_kernel(qaug_ref, kaug_ref, v_tb_ref, o_ref, s0_ref, s1_ref, m_ref, acc_ref,
                        *, head_dim, hps):
    j = pl.program_id(2)
    blk = MOBA_BLOCK
    key_i = lax.broadcasted_iota(jnp.int32, (blk, blk), 0)
    qry_i = lax.broadcasted_iota(jnp.int32, (blk, blk), 1)

    def scores(n, s_ref, own):
        for hh in range(hps):
            k_blk = kaug_ref[0, hh, pl.ds(pl.multiple_of(n * blk, blk), blk), :]
            s = _dot(k_blk, qaug_ref[0, hh])
            if own:
                s = jnp.where(key_i <= qry_i, s, NEG_INF)
            s_ref[hh] = s

    def probs(s_ref):
        out = []
        for hh in range(hps):
            s = s_ref[hh]
            m = m_ref[hh]
            m_new = jnp.maximum(m, jnp.max(s, axis=0, keepdims=True))
            m_ref[hh] = m_new
            out.append((jnp.exp2(m - m_new), jnp.exp2(s - m_new).astype(BF16)))
        return out

    def values(n, alpha_p):
        for hh, (alpha, p) in enumerate(alpha_p):
            pv = _dot(v_tb_ref[0, n, hh * V_ROWS:(hh + 1) * V_ROWS, :], p)
            acc_ref[hh] = alpha * acc_ref[hh] + pv

    def accumulate(s_ref, n):
        values(n, probs(s_ref))

    m_ref[...] = jnp.full(m_ref.shape, BELOW_NEG_INF, F32)
    acc_ref[...] = jnp.zeros(acc_ref.shape, F32)
    scores(j, s0_ref, own=True)

    bufs = (s0_ref, s1_ref)
    unroll = ATTN_BLOCKS_PER_ITER

    def run(first, count, pending):
        for t in range(count):
            scores(first + t, bufs[(t + 1) % 2], own=False)
            accumulate(bufs[t % 2], pending)
            pending = first + t
        return pending

    def group(i, carry):
        run(unroll * i, unroll, jnp.where(i == 0, j, unroll * i - 1))
        return carry

    n_groups = j // unroll
    lax.fori_loop(0, n_groups, group, 0)
    base = n_groups * unroll
    for rem in range(unroll):
        @pl.when(j - base == rem)
        def _(rem=rem):
            pending = run(base, rem, jnp.where(base == 0, j, base - 1))
            accumulate(bufs[rem % 2], pending)

    o_t = jnp.concatenate(
        [acc_ref[hh, 0:head_dim, :] / acc_ref[hh, head_dim:head_dim + 1, :] for hh in range(hps)], axis=0)
    o_ref[0] = o_t.T.astype(o_ref.dtype)


def _moba_prompt_call(qaug, kaug, v_tb, *, n_heads, head_dim):
    n_seq, _, seq_len, hw = kaug.shape
    n_blocks = seq_len // MOBA_BLOCK
    blk = MOBA_BLOCK
    hps = ATTN_HEADS_PER_STEP
    assert n_heads % hps == 0 and (hps * head_dim) % 128 == 0 and ATTN_BLOCKS_PER_ITER % 2 == 0
    resident = pl.Buffered(1)
    return pl.pallas_call(
        functools.partial(_moba_prompt_kernel, head_dim=head_dim, hps=hps),
        grid=(n_seq, n_heads // hps, n_blocks),
        in_specs=[
            pl.BlockSpec((1, hps, hw, blk), lambda b, hp, j: (b, hp, 0, j)),
            pl.BlockSpec((1, hps, seq_len, hw), lambda b, hp, j: (b, hp, 0, 0), pipeline_mode=resident),
            pl.BlockSpec((1, n_blocks, hps * V_ROWS, blk), lambda b, hp, j: (b, 0, hp, 0),
                         pipeline_mode=resident),
        ],
        out_specs=pl.BlockSpec((1, blk, hps * head_dim), lambda b, hp, j: (b, j, hp)),
        out_shape=jax.ShapeDtypeStruct((n_seq, seq_len, n_heads * head_dim), BF16),
        scratch_shapes=[pltpu.VMEM((hps, blk, blk), F32),
                        pltpu.VMEM((hps, blk, blk), F32),
                        pltpu.VMEM((hps, 1, blk), F32),
                        pltpu.VMEM((hps, V_ROWS, blk), F32)],
        compiler_params=pltpu.CompilerParams(
            dimension_semantics=("arbitrary", "arbitrary", "arbitrary"),
            vmem_limit_bytes=VMEM_LIMIT_BYTES),
        name="moba_prompt",
    )(qaug, kaug, v_tb)


def _sample_scores_kernel(pt_ref, qb_ref, ck_hbm, sc_ref, buf_ref, sem_ref, *, ch, n_heads, head_dim):
    b = pl.program_id(0)
    c = pl.program_id(1)
    nc = pl.num_programs(1)
    step = b * nc + c
    total = pl.num_programs(0) * nc

    def page_copy(seq, chunk, i, slot):
        page = pt_ref[seq, chunk * ch + i]
        return pltpu.make_async_copy(ck_hbm.at[page], buf_ref.at[slot, i], sem_ref.at[slot])

    def start(next_step):
        seq = next_step // nc
        chunk = next_step % nc
        for i in range(ch):
            page_copy(seq, chunk, i, next_step % 2).start()

    @pl.when(step == 0)
    def _():
        start(step)

    @pl.when(step + 1 < total)
    def _():
        start(step + 1)

    slot = step % 2
    for i in range(ch):
        page_copy(b, c, i, slot).wait()

    def body(i, carry):
        rows = []
        for h in range(n_heads):
            prod = buf_ref[slot, i, h] * qb_ref[0, h * head_dim:(h + 1) * head_dim, :]
            rows.append(jnp.sum(prod, axis=0, keepdims=True))
        sc_ref[0, i] = jnp.concatenate(rows, axis=0)
        return carry

    lax.fori_loop(0, ch, body, 0)


def _sample_probs_kernel(sc_ref, sself_ref, slopes_ref, p_ref, pself_ref, sel_ref,
                         *, n_pages, page_size, past_len, head_dim, n_heads):
    pages_per_block = MOBA_BLOCK // page_size
    n_blocks = n_pages // pages_per_block
    lanes = sself_ref.shape[-1]
    lane_i = lax.broadcasted_iota(jnp.int32, (n_heads, lanes), 1)
    scale = head_dim ** -0.5
    slope = slopes_ref[...]
    pos_in_page = lax.broadcasted_iota(jnp.int32, (n_heads, page_size), 1)

    def one_sequence(bb):
        gate = jnp.full((n_heads, lanes), NEG_INF, F32)
        for n in range(n_blocks):
            tot = sc_ref[bb, n * pages_per_block]
            for pg in range(1, pages_per_block):
                tot = tot + sc_ref[bb, n * pages_per_block + pg]
            g_n = jnp.sum(tot, axis=1, keepdims=True) * (1.0 / MOBA_BLOCK)
            gate = jnp.where(lane_i == n, g_n, gate)
        picks = []
        for _ in range(MOBA_TOPK):
            mx = jnp.max(gate, axis=1, keepdims=True)
            idx = jnp.min(jnp.where(gate == mx, lane_i, lanes), axis=1, keepdims=True)
            picks.append(idx)
            gate = jnp.where(lane_i == idx, BELOW_NEG_INF, gate)
        sel_vec = jnp.zeros((n_heads, lanes), jnp.int32)
        for t, idx in enumerate(picks):
            sel_vec = jnp.where(lane_i == t, idx, sel_vec)
        sel_ref[bb] = sel_vec

        s_self = sself_ref[bb] * scale

        def page_scores(pg):
            selected = picks[0] == (pg // pages_per_block)
            for idx in picks[1:]:
                selected = selected | (idx == (pg // pages_per_block))
            dist = (past_len - pg * page_size - pos_in_page).astype(F32)
            s = sc_ref[bb, pg] * scale - slope[:, 0:page_size] * dist
            return jnp.where(selected, s, NEG_INF)

        top = page_scores(0)
        for pg in range(1, n_pages):
            top = jnp.maximum(top, page_scores(pg))
        m = jnp.maximum(s_self, jnp.max(top, axis=1, keepdims=True))
        e_self = jnp.exp(s_self - m)
        tot = jnp.zeros((n_heads, page_size), F32)
        for pg in range(n_pages):
            e = jnp.exp(page_scores(pg) - m[:, 0:page_size])
            p_ref[bb, pg] = e
            tot = tot + e
        inv = 1.0 / (e_self + jnp.sum(tot, axis=1, keepdims=True))
        for pg in range(n_pages):
            p_ref[bb, pg] = p_ref[bb, pg] * inv[:, 0:page_size]
        pself_ref[bb] = e_self * inv

    for bb in range(sc_ref.shape[0]):
        one_sequence(bb)


def _sample_pv_kernel(pt_ref, sel_ref, p_ref, pself_e_ref, v_t_ref, cv_hbm, o_ref, vbuf_ref, sem_ref,
                      *, n_heads, head_dim, pages_per_block):
    b = pl.program_id(0)
    nb = pl.num_programs(0)
    per_head = MOBA_TOPK * pages_per_block

    def tile_copy(seq, h, t, pg, slot):
        blk = sel_ref[seq, h * MOBA_TOPK + t]
        page = pt_ref[seq, blk * pages_per_block + pg]
        k = h * per_head + t * pages_per_block + pg
        return pltpu.make_async_copy(cv_hbm.at[page, h], vbuf_ref.at[slot, k], sem_ref.at[slot])

    def for_all_tiles(seq, slot, fn):
        for h in range(n_heads):
            for t in range(MOBA_TOPK):
                for pg in range(pages_per_block):
                    fn(tile_copy(seq, h, t, pg, slot))

    @pl.when(b == 0)
    def _():
        for_all_tiles(b, 0, lambda cp: cp.start())
        o_ref[...] = jnp.zeros(o_ref.shape, F32)

    @pl.when(b + 1 < nb)
    def _():
        for_all_tiles(b + 1, (b + 1) % 2, lambda cp: cp.start())

    slot = b % 2
    for_all_tiles(b, slot, lambda cp: cp.wait())

    cols = []
    for h in range(n_heads):
        acc = jnp.zeros((head_dim, vbuf_ref.shape[-1]), F32)
        for t in range(MOBA_TOPK):
            blk = sel_ref[b, h * MOBA_TOPK + t]
            for pg in range(pages_per_block):
                k = h * per_head + t * pages_per_block + pg
                p_row = p_ref[0, blk * pages_per_block + pg, h:h + 1, :]
                acc = acc + vbuf_ref[slot, k] * p_row
        cols.append(jnp.sum(acc, axis=1, keepdims=True))
    col = jnp.concatenate(cols, axis=0)
    seq_i = lax.broadcasted_iota(jnp.int32, o_ref.shape, 1)
    o_ref[...] = jnp.where(seq_i == b, col, o_ref[...])

    @pl.when(b == nb - 1)
    def _():
        o_ref[...] = o_ref[...] + pself_e_ref[...] * v_t_ref[...]


def _moba_sample(q_t, v_t, sself, slopes, cache_k, cache_v, page_table, *, n_heads, head_dim):
    d_attn, n_seq = q_t.shape
    n_phys, page_size = cache_k.shape[0], cache_k.shape[1]
    n_pages = page_table.shape[1]
    past_len = n_pages * page_size
    pages_per_block = MOBA_BLOCK // page_size
    assert MOBA_BLOCK % page_size == 0 and past_len % MOBA_BLOCK == 0
    assert past_len // MOBA_BLOCK >= MOBA_TOPK
    ch = min(PAGES_PER_CHUNK, n_pages)
    assert n_pages % ch == 0
    ck = jnp.transpose(cache_k, (0, 2, 3, 1))
    cv = jnp.transpose(cache_v, (0, 2, 3, 1))
    qb = jnp.broadcast_to(q_t.T[:, :, None], (n_seq, d_attn, page_size))

    scores = pl.pallas_call(
        functools.partial(_sample_scores_kernel, ch=ch, n_heads=n_heads, head_dim=head_dim),
        grid_spec=pltpu.PrefetchScalarGridSpec(
            num_scalar_prefetch=1,
            grid=(n_seq, n_pages // ch),
            in_specs=[pl.BlockSpec((1, d_attn, page_size), lambda b, c, pt: (b, 0, 0)),
                      pl.BlockSpec(memory_space=pl.ANY)],
            out_specs=pl.BlockSpec((1, ch, n_heads, page_size), lambda b, c, pt: (b, c, 0, 0)),
            scratch_shapes=[pltpu.VMEM((2, ch, n_heads, head_dim, page_size), F32),
                            pltpu.SemaphoreType.DMA((2,))]),
        out_shape=jax.ShapeDtypeStruct((n_seq, n_pages, n_heads, page_size), F32),
        compiler_params=pltpu.CompilerParams(dimension_semantics=("arbitrary", "arbitrary"),
                                             vmem_limit_bytes=VMEM_LIMIT_BYTES),
        name="sample_scores",
    )(page_table, qb, ck)

    lanes = 128
    sb = SAMPLE_SEQS_PER_STEP if n_seq % SAMPLE_SEQS_PER_STEP == 0 else 1
    sself_b = jnp.broadcast_to(sself.T[:, :, None], (n_seq, n_heads, lanes))
    slopes_b = jnp.broadcast_to(slopes[:, None], (n_heads, lanes))
    probs, pself, sel = pl.pallas_call(
        functools.partial(_sample_probs_kernel, n_pages=n_pages, page_size=page_size,
                          past_len=past_len, head_dim=head_dim, n_heads=n_heads),
        grid=(n_seq // sb,),
        in_specs=[pl.BlockSpec((sb, n_pages, n_heads, page_size), lambda b: (b, 0, 0, 0)),
                  pl.BlockSpec((sb, n_heads, lanes), lambda b: (b, 0, 0)),
                  pl.BlockSpec((n_heads, lanes), lambda b: (0, 0))],
        out_specs=[pl.BlockSpec((sb, n_pages, n_heads, page_size), lambda b: (b, 0, 0, 0)),
                   pl.BlockSpec((sb, n_heads, lanes), lambda b: (b, 0, 0)),
                   pl.BlockSpec((sb, n_heads, lanes), lambda b: (b, 0, 0))],
        out_shape=[jax.ShapeDtypeStruct((n_seq, n_pages, n_heads, page_size), F32),
                   jax.ShapeDtypeStruct((n_seq, n_heads, lanes), F32),
                   jax.ShapeDtypeStruct((n_seq, n_heads, lanes), jnp.int32)],
        compiler_params=pltpu.CompilerParams(dimension_semantics=("arbitrary",)),
        name="sample_probs",
    )(scores, sself_b, slopes_b)

    sel_flat = sel[:, :, :MOBA_TOPK].reshape(n_seq, n_heads * MOBA_TOPK)
    pself_e = jnp.repeat(pself[:, :, 0].T, head_dim, axis=0)
    n_tiles = n_heads * MOBA_TOPK * pages_per_block
    return pl.pallas_call(
        functools.partial(_sample_pv_kernel, n_heads=n_heads, head_dim=head_dim,
                          pages_per_block=pages_per_block),
        grid_spec=pltpu.PrefetchScalarGridSpec(
            num_scalar_prefetch=2,
            grid=(n_seq,),
            in_specs=[pl.BlockSpec((1, n_pages, n_heads, page_size), lambda b, pt, sl: (b, 0, 0, 0)),
                      pl.BlockSpec((d_attn, n_seq), lambda b, pt, sl: (0, 0)),
                      pl.BlockSpec((d_attn, n_seq), lambda b, pt, sl: (0, 0)),
                      pl.BlockSpec(memory_space=pl.ANY)],
            out_specs=pl.BlockSpec((d_attn, n_seq), lambda b, pt, sl: (0, 0)),
            scratch_shapes=[pltpu.VMEM((2, n_tiles, head_dim, page_size), F32),
                            pltpu.SemaphoreType.DMA((2,))]),
        out_shape=jax.ShapeDtypeStruct((d_attn, n_seq), F32),
        compiler_params=pltpu.CompilerParams(dimension_semantics=("arbitrary",)),
        name="sample_pv",
    )(page_table, sel_flat, probs, pself_e, v_t, cv)


def _post_kernel(attn_ref, siga_ref, gconv_ref, x_ref, p_ref, wao_ref, wo_ref, n2_ref, wg_ref, wu_ref,
                 wd_ref, n3_ref, wple_ref, wpg_ref, y_ref, *, ff_chunks):
    y_attn = _dot(attn_ref[...], wao_ref[...])
    merged = siga_ref[...].astype(F32) * y_attn + gconv_ref[...].astype(F32)
    x1 = x_ref[...] + _dot(merged.astype(BF16), wo_ref[...])
    h2 = _rms_norm(x1, n2_ref[...]).astype(BF16)
    x2 = x1
    for lo, hi in ff_chunks:
        g = _dot(h2, wg_ref[:, lo:hi])
        u = _dot(h2, wu_ref[:, lo:hi])
        a = (g * jax.nn.sigmoid(g) * u).astype(BF16)
        x2 = x2 + _dot(a, wd_ref[lo:hi, :])
    h3 = _rms_norm(x2, n3_ref[...]).astype(BF16)
    ple = _dot(p_ref[...].astype(BF16), wple_ref[...])
    y_ref[...] = x2 + ple * jax.nn.sigmoid(_dot(h3, wpg_ref[...]))


def _post_call(attn, siga, gconv, x2d, p2d, weights, *, tm):
    (wao, wo, n2, wg, wu, wd, n3, wple, wpg) = weights
    rows, d_model = x2d.shape
    d_ff = wg.shape[1]
    chunk = 1024
    ff_chunks = tuple((lo, min(lo + chunk, d_ff)) for lo in range(0, d_ff, chunk))
    row = lambda w: pl.BlockSpec((tm, w), lambda i: (i, 0))
    return pl.pallas_call(
        functools.partial(_post_kernel, ff_chunks=ff_chunks),
        grid=(rows // tm,),
        in_specs=[row(attn.shape[1]), row(d_model), row(d_model), row(d_model), row(p2d.shape[1])]
                 + [_const_spec(w.shape) for w in weights],
        out_specs=row(d_model),
        out_shape=jax.ShapeDtypeStruct((rows, d_model), F32),
        compiler_params=pltpu.CompilerParams(dimension_semantics=("arbitrary",),
                                             vmem_limit_bytes=VMEM_LIMIT_BYTES),
        name="post",
    )(attn, siga, gconv, x2d, p2d, *weights)


def kernel(x_prompt, x_sample, p_prompt, p_sample, cache_k, cache_v, state_conv, page_table, norm1, w_in, q_norm, k_norm, conv_w, w_attn_out, w_conv_out, w_o, norm2, w_gate, w_up, w_down, norm3, w_ple, w_ple_gate):
    depth = w_in.shape[0]
    n_seq, seq_len, d_model = x_prompt.shape
    dec_seq, dec_len, _ = x_sample.shape
    head_dim = q_norm.shape[-1]
    d_attn = w_attn_out.shape[1]
    d_conv = w_conv_out.shape[1]
    n_heads = d_attn // head_dim
    n_blocks = seq_len // MOBA_BLOCK
    assert dec_len == 1, "sample path handles one new token per sequence"
    assert conv_w.shape[1] == CONV_WIDTH
    assert seq_len % ROW_TILE == 0 and ROW_TILE % MOBA_BLOCK == 0
    assert MOBA_BLOCK == 1 << MOBA_BLOCK_LOG2
    assert n_blocks <= AUG_MAX_BLOCKS and AUG_END <= head_dim and head_dim < V_ROWS
    slopes = 2.0 ** (-8.0 * jnp.arange(1, n_heads + 1, dtype=F32) / n_heads)

    yp = x_prompt
    ys = x_sample.reshape(dec_seq, d_model)
    outs = [[] for _ in range(6)]
    for i in range(depth):
        proj_w = (norm1[i][None, :],
                  w_in[i][:, :3 * d_attn].T.astype(BF16),
                  w_in[i][:, 3 * d_attn:].astype(BF16),
                  jnp.tile(q_norm[i], n_heads)[:, None],
                  jnp.tile(k_norm[i], n_heads)[:, None],
                  conv_w[i], w_conv_out[i].astype(BF16), n_heads, head_dim)
        post_w = (w_attn_out[i].astype(BF16), w_o[i].astype(BF16), norm2[i][None, :],
                  w_gate[i].astype(BF16), w_up[i].astype(BF16), w_down[i].astype(BF16),
                  norm3[i][None, :], w_ple[i].astype(BF16), w_ple_gate[i].astype(BF16))

        q_t, k_t, v_t, v_tb, kaug, kbar, gconv, siga, tails = _proj_call(
            yp, None, proj_w, seq_mode=True, tm=ROW_TILE)
        kbar_h = jnp.transpose(kbar[..., :head_dim], (0, 2, 1, 3, 4)).reshape(
            n_seq, n_heads, n_blocks, head_dim)
        kbar_h = jnp.pad(kbar_h, ((0, 0), (0, 0), (0, AUG_MAX_BLOCKS - n_blocks), (0, 0)))
        qaug = _select_call(slopes, q_t, kbar_h, n_heads=n_heads, head_dim=head_dim)
        attn = _moba_prompt_call(qaug, kaug, v_tb, n_heads=n_heads, head_dim=head_dim)
        yp = _post_call(attn.reshape(n_seq * seq_len, d_attn), siga, gconv,
                        yp.reshape(n_seq * seq_len, d_model),
                        p_prompt[i].reshape(n_seq * seq_len, -1), post_w,
                        tm=ROW_TILE).reshape(n_seq, seq_len, d_model)
        to_cache_layout = lambda t, n, l: jnp.transpose(
            t.reshape(n, n_heads, head_dim, l), (0, 3, 1, 2))
        outs[0].append(to_cache_layout(k_t, n_seq, seq_len))
        outs[1].append(to_cache_layout(v_t, n_seq, seq_len))
        tiles_per_seq = seq_len // ROW_TILE
        outs[2].append(tails.reshape(n_seq, tiles_per_seq, CONV_WIDTH - 1, d_conv)[:, -1])

        hist = state_conv[i]
        sq_t, sk_t, sv_t, sself, sgconv, ssiga, su = _proj_call(
            ys[None], (hist[:, 0], hist[:, 1]), proj_w, seq_mode=False, tm=dec_seq)
        o_t = _moba_sample(sq_t[0], sv_t[0], sself, slopes, cache_k[i], cache_v[i], page_table,
                           n_heads=n_heads, head_dim=head_dim)
        ys = _post_call(o_t.T.astype(BF16), ssiga, sgconv, ys, p_sample[i].reshape(dec_seq, -1),
                        post_w, tm=dec_seq)
        outs[3].append(to_cache_layout(sk_t, 1, dec_seq).reshape(dec_seq, 1, n_heads, head_dim))
        outs[4].append(to_cache_layout(sv_t, 1, dec_seq).reshape(dec_seq, 1, n_heads, head_dim))
        outs[5].append(jnp.stack([hist[:, 1], su], axis=1))

    return (yp, ys.reshape(dec_seq, 1, d_model), *[jnp.stack(o) for o in outs])
```

```python
import functools

import jax
import jax.numpy as jnp
from jax import lax
from jax.experimental import pallas as pl
from jax.experimental.pallas import tpu as pltpu

RMS_EPS = 1e-6
NEG_INF = -1e30
BELOW_NEG_INF = -3e38
MOBA_BLOCK = 256
MOBA_BLOCK_LOG2 = 8
MOBA_TOPK = 3
CONV_WIDTH = 3
LOG2E = 1.4426950408889634
AUG_MAX_BLOCKS = 32
SPLIT = 3
AUG_KBLK = AUG_MAX_BLOCKS
AUG_KOFF = AUG_KBLK + SPLIT
AUG_ONE = AUG_KOFF + SPLIT
AUG_END = AUG_ONE + SPLIT
V_ROWS = 80
ATTN_HEADS_PER_STEP = 8
ATTN_BLOCKS_PER_ITER = 4
SUBLANES = 8
VMEM_LIMIT_BYTES = 60 * 1024 * 1024
ROW_TILE = 512
SELECT_TILE = 2048
SAMPLE_SEQS_PER_STEP = 4
PAGES_PER_CHUNK = 32

F32 = jnp.float32
BF16 = jnp.bfloat16


def _dot(a, b):
    return jnp.dot(a, b, preferred_element_type=F32)


def _dot_nt(a, b):
    return lax.dot_general(a, b, (((1,), (1,)), ((), ())), preferred_element_type=F32)


def _split_bf16(a):
    hi = a.astype(BF16)
    lo = (a - hi.astype(F32)).astype(BF16)
    return hi, lo


def _dot_3pass(a, b):
    ah, al = _split_bf16(a)
    bh, bl = _split_bf16(b)
    return _dot(ah, bh) + (_dot(ah, bl) + _dot(al, bh))


def _rms_norm(x, w):
    ms = jnp.mean(x * x, axis=-1, keepdims=True)
    return x * lax.rsqrt(ms + RMS_EPS) * w


def _head_norm_t(z_t, w_col, n_heads, head_dim):
    tokens = z_t.shape[1]
    z3 = z_t.reshape(n_heads, head_dim, tokens)
    ms = jnp.mean(z3 * z3, axis=1, keepdims=True)
    y = (z3 * lax.rsqrt(ms + RMS_EPS)).reshape(n_heads * head_dim, tokens)
    return y * w_col


def _const_spec(shape):
    zeros = (0,) * len(shape)
    return pl.BlockSpec(shape, lambda *_: zeros, pipeline_mode=pl.Buffered(1))


def _key_aug_rows(pos0, tokens, rows):
    r = lax.broadcasted_iota(jnp.int32, (rows, tokens), 0)
    pos = pos0 + lax.broadcasted_iota(jnp.int32, (rows, tokens), 1)
    block = jnp.right_shift(pos, MOBA_BLOCK_LOG2)
    offset = jnp.bitwise_and(pos, MOBA_BLOCK - 1).astype(F32)
    one = (r == block) | ((r >= AUG_ONE) & (r < AUG_END))
    is_blk = (r >= AUG_KBLK) & (r < AUG_KOFF)
    is_off = (r >= AUG_KOFF) & (r < AUG_ONE)
    return jnp.where(one, 1.0, jnp.where(is_blk, block.astype(F32), jnp.where(is_off, offset, 0.0)))


def _split3_bf16(a):
    p0 = a.astype(BF16).astype(F32)
    p1 = (a - p0).astype(BF16).astype(F32)
    p2 = (a - p0 - p1).astype(BF16).astype(F32)
    return p0, p1, p2


def _proj_kernel(*refs, seq_mode, tm, n_heads, head_dim, d_attn, d_conv, d_model):
    if seq_mode:
        (x_ref, n1_ref, wqkv_t_ref, wrest_ref, qn_ref, kn_ref, convw_ref, wco_ref,
         q_t_ref, k_t_ref, v_t_ref, v_tb_ref, kaug_ref, kbar_ref, gconv_ref, siga_ref, tail_ref,
         ubuf_ref) = refs
    else:
        (x_ref, h0_ref, h1_ref, n1_ref, wqkv_t_ref, wrest_ref, qn_ref, kn_ref, convw_ref, wco_ref,
         q_t_ref, k_t_ref, v_t_ref, sself_ref, gconv_ref, siga_ref, u_ref) = refs

    x = x_ref[0]
    h = _rms_norm(x, n1_ref[...]).astype(BF16)

    q_t = _head_norm_t(_dot_nt(wqkv_t_ref[0:d_attn, :], h), qn_ref[...], n_heads, head_dim)
    k_t = _head_norm_t(_dot_nt(wqkv_t_ref[d_attn:2 * d_attn, :], h), kn_ref[...], n_heads, head_dim)
    v_t = _dot_nt(wqkv_t_ref[2 * d_attn:3 * d_attn, :], h)
    q_t_ref[0] = q_t
    k_t_ref[0] = k_t
    v_t_ref[0] = v_t

    if seq_mode:
        aug_t = _key_aug_rows(pl.program_id(1) * tm, tm, head_dim)
        for hd in range(n_heads):
            k_h = jnp.concatenate([k_t[hd * head_dim:(hd + 1) * head_dim], aug_t], axis=0).T
            kaug_ref[0, hd] = k_h.astype(BF16)
            for i in range(tm // MOBA_BLOCK):
                kbar_ref[0, 0, hd, i:i + 1, :] = jnp.mean(
                    k_h[i * MOBA_BLOCK:(i + 1) * MOBA_BLOCK], axis=0, keepdims=True)
        pad_r = lax.broadcasted_iota(jnp.int32, (V_ROWS - head_dim, MOBA_BLOCK), 0)
        ones_pad = jnp.where(pad_r == 0, 1.0, 0.0)
        for i in range(tm // MOBA_BLOCK):
            lo = i * MOBA_BLOCK
            for hd in range(n_heads):
                v_h = v_t[hd * head_dim:(hd + 1) * head_dim, lo:lo + MOBA_BLOCK]
                v_tb_ref[0, i, hd * V_ROWS:(hd + 1) * V_ROWS, :] = jnp.concatenate(
                    [v_h, ones_pad], axis=0).astype(BF16)
    else:
        qk = (q_t * k_t).reshape(n_heads, head_dim, tm)
        sself_ref[...] = jnp.sum(qk, axis=1)

    zc = _dot(h, wrest_ref[:, 0:3 * d_conv])
    cb = zc[:, 0:d_conv]
    u = zc[:, d_conv:2 * d_conv] * zc[:, 2 * d_conv:3 * d_conv]
    if seq_mode:
        @pl.when(pl.program_id(1) == 0)
        def _():
            ubuf_ref[0:SUBLANES, :] = jnp.zeros((SUBLANES, d_conv), F32)
        ubuf_ref[SUBLANES:SUBLANES + tm, :] = u
        u_m2 = ubuf_ref[SUBLANES - 2:SUBLANES - 2 + tm, :]
        u_m1 = ubuf_ref[SUBLANES - 1:SUBLANES - 1 + tm, :]
        tail_ref[0] = u[tm - (CONV_WIDTH - 1):tm, :]
        ubuf_ref[0:SUBLANES, :] = u[tm - SUBLANES:tm, :]
    else:
        u_m2 = h0_ref[...]
        u_m1 = h1_ref[...]
        u_ref[...] = u
    conv = convw_ref[0:1, :] * u_m2 + convw_ref[1:2, :] * u_m1 + convw_ref[2:3, :] * u
    y_conv = _dot((cb * conv).astype(BF16), wco_ref[...])

    zg = _dot(h, wrest_ref[:, 3 * d_conv:3 * d_conv + 2 * d_model])
    siga_ref[...] = jax.nn.sigmoid(zg[:, 0:d_model]).astype(BF16)
    gconv_ref[...] = (jax.nn.sigmoid(zg[:, d_model:2 * d_model]) * y_conv).astype(BF16)


def _proj_call(x3, hist, weights, *, seq_mode, tm):
    (n1, wqkv_t, wrest, qn_col, kn_col, convw, wco, n_heads, head_dim) = weights
    n_seq, seq_len, d_model = x3.shape
    d_attn = n_heads * head_dim
    d_conv = convw.shape[1]
    n_tiles = seq_len // tm
    rows = n_seq * seq_len
    kern = functools.partial(_proj_kernel, seq_mode=seq_mode, tm=tm, n_heads=n_heads,
                             head_dim=head_dim, d_attn=d_attn, d_conv=d_conv, d_model=d_model)
    row_map = lambda b, t: (b * n_tiles + t, 0)
    w_specs = [_const_spec(n1.shape), _const_spec(wqkv_t.shape), _const_spec(wrest.shape),
               _const_spec(qn_col.shape), _const_spec(kn_col.shape), _const_spec(convw.shape),
               _const_spec(wco.shape)]
    t_spec = pl.BlockSpec((1, d_attn, tm), lambda b, t: (b, 0, t))
    t_shape = jax.ShapeDtypeStruct((n_seq, d_attn, seq_len), F32)
    gate_specs = [pl.BlockSpec((tm, d_model), row_map)] * 2
    gate_shapes = [jax.ShapeDtypeStruct((rows, d_model), BF16)] * 2
    x_spec = pl.BlockSpec((1, tm, d_model), lambda b, t: (b, t, 0))
    if seq_mode:
        nb_t = tm // MOBA_BLOCK
        in_specs = [x_spec] + w_specs
        args = (x3,)
        out_specs = [t_spec, t_spec, t_spec,
                     pl.BlockSpec((1, nb_t, n_heads * V_ROWS, MOBA_BLOCK), lambda b, t: (b, t, 0, 0)),
                     pl.BlockSpec((1, n_heads, tm, 2 * head_dim), lambda b, t: (b, 0, t, 0)),
                     pl.BlockSpec((1, 1, n_heads, nb_t, 2 * head_dim), lambda b, t: (b, t, 0, 0, 0)),
                     *gate_specs,
                     pl.BlockSpec((1, CONV_WIDTH - 1, d_conv), lambda b, t: (b * n_tiles + t, 0, 0))]
        out_shape = [t_shape, t_shape, t_shape,
                     jax.ShapeDtypeStruct((n_seq, seq_len // MOBA_BLOCK, n_heads * V_ROWS, MOBA_BLOCK),
                                          BF16),
                     jax.ShapeDtypeStruct((n_seq, n_heads, seq_len, 2 * head_dim), BF16),
                     jax.ShapeDtypeStruct((n_seq, n_tiles, n_heads, nb_t, 2 * head_dim), F32),
                     *gate_shapes,
                     jax.ShapeDtypeStruct((n_seq * n_tiles, CONV_WIDTH - 1, d_conv), F32)]
        scratch = [pltpu.VMEM((SUBLANES + tm, d_conv), F32)]
    else:
        h_spec = pl.BlockSpec((tm, d_conv), row_map)
        in_specs = [x_spec, h_spec, h_spec] + w_specs
        args = (x3, hist[0], hist[1])
        out_specs = [t_spec, t_spec, t_spec,
                     pl.BlockSpec((n_heads, tm), lambda b, t: (0, b * n_tiles + t)),
                     *gate_specs,
                     pl.BlockSpec((tm, d_conv), row_map)]
        out_shape = [t_shape, t_shape, t_shape,
                     jax.ShapeDtypeStruct((n_heads, rows), F32),
                     *gate_shapes,
                     jax.ShapeDtypeStruct((rows, d_conv), F32)]
        scratch = []
    return pl.pallas_call(
        kern, grid=(n_seq, n_tiles), in_specs=in_specs, out_specs=out_specs, out_shape=out_shape,
        scratch_shapes=scratch,
        compiler_params=pltpu.CompilerParams(dimension_semantics=("arbitrary", "arbitrary"),
                                             vmem_limit_bytes=VMEM_LIMIT_BYTES),
        name="proj_seq" if seq_mode else "proj_rows",
    )(*args, n1, wqkv_t, wrest, qn_col, kn_col, convw, wco)


def _topk_rows(gate, row_iota, k):
    n = gate.shape[0]
    picks = []
    for _ in range(k):
        mx = jnp.max(gate, axis=0, keepdims=True)
        idx = jnp.min(jnp.where(gate == mx, row_iota, n), axis=0, keepdims=True)
        picks.append(idx)
        gate = jnp.where(row_iota == idx, BELOW_NEG_INF, gate)
    return picks


def _select_kernel(slopes_ref, q_t_ref, kbar_ref, qaug_ref, *, head_dim, tq):
    hd = pl.program_id(1)
    pos0 = pl.program_id(2) * tq
    slope = slopes_ref[hd]
    q = q_t_ref[0]
    gate = _dot_3pass(kbar_ref[0, 0], q)
    blk_i = lax.broadcasted_iota(jnp.int32, (AUG_MAX_BLOCKS, tq), 0)
    pos = pos0 + lax.broadcasted_iota(jnp.int32, (AUG_MAX_BLOCKS, tq), 1)
    q_blk = jnp.right_shift(pos, MOBA_BLOCK_LOG2)
    past = blk_i < q_blk
    gate = jnp.where(past, gate, NEG_INF)
    mask = jnp.where(blk_i == q_blk, 0.0, NEG_INF)
    for idx in _topk_rows(gate, blk_i, MOBA_TOPK):
        mask = jnp.where((blk_i == idx) & past, 0.0, mask)
    tail_rows = head_dim - AUG_MAX_BLOCKS
    r = AUG_MAX_BLOCKS + lax.broadcasted_iota(jnp.int32, (tail_rows, tq), 0)
    q_pos = (pos0 + lax.broadcasted_iota(jnp.int32, (tail_rows, tq), 1)).astype(F32)
    c = jnp.full((tail_rows, tq), slope * LOG2E, F32)
    c_parts = _split3_bf16(c)
    v_parts = _split3_bf16(-(c * q_pos))
    tail = jnp.zeros((tail_rows, tq), F32)
    for i in range(SPLIT):
        tail = jnp.where(r == AUG_KBLK + i, MOBA_BLOCK * c_parts[i], tail)
        tail = jnp.where(r == AUG_KOFF + i, c_parts[i], tail)
        tail = jnp.where(r == AUG_ONE + i, v_parts[i], tail)
    scale = head_dim ** -0.5 * LOG2E
    qaug_ref[0, 0] = jnp.concatenate([q * scale, mask, tail], axis=0).astype(BF16)


def _select_call(slopes, q_t, kbar, *, n_heads, head_dim):
    n_seq, _, seq_len = q_t.shape
    tq = min(SELECT_TILE, seq_len)
    assert seq_len % tq == 0 and tq % MOBA_BLOCK == 0
    return pl.pallas_call(
        functools.partial(_select_kernel, head_dim=head_dim, tq=tq),
        grid=(n_seq, n_heads, seq_len // tq),
        in_specs=[pl.BlockSpec(memory_space=pltpu.SMEM),
                  pl.BlockSpec((1, head_dim, tq), lambda b, h, c: (b, h, c)),
                  pl.BlockSpec((1, 1, AUG_MAX_BLOCKS, head_dim), lambda b, h, c: (b, h, 0, 0))],
        out_specs=pl.BlockSpec((1, 1, 2 * head_dim, tq), lambda b, h, c: (b, h, 0, c)),
        out_shape=jax.ShapeDtypeStruct((n_seq, n_heads, 2 * head_dim, seq_len), BF16),
        compiler_params=pltpu.CompilerParams(
            dimension_semantics=("arbitrary", "arbitrary", "arbitrary")),
        name="moba_select",
    )(slopes, q_t, kbar)


def _stream_page_scores(step, total, nc, pt_ref, qb_ref, ck_hbm, sc_ref, buf_ref, sem_ref,
                        *, ch, n_heads, head_dim, unrolled):
    def page_copy(at_step, i):
        page = pt_ref[at_step // nc, (at_step % nc) * ch + i]
        slot = at_step % 2
        return pltpu.make_async_copy(ck_hbm.at[page], buf_ref.at[slot, i], sem_ref.at[slot])

    @pl.when(step == 0)
    def _():
        for i in range(ch):
            page_copy(step, i).start()

    @pl.when(step + 1 < total)
    def _():
        for i in range(ch):
            page_copy(step + 1, i).start()

    for i in range(ch):
        page_copy(step, i).wait()
    slot = step % 2

    def page(i):
        rows = []
        for h in range(n_heads):
            prod = buf_ref[slot, i, h] * qb_ref[0, h * head_dim:(h + 1) * head_dim, :]
            rows.append(jnp.sum(prod, axis=0, keepdims=True))
        sc_ref[0, i] = jnp.concatenate(rows, axis=0)

    if unrolled:
        for i in range(ch):
            page(i)
    else:
        def body(i, carry):
            page(i)
            return carry
        lax.fori_loop(0, ch, body, 0)


def _moba_prompt_kernel(*refs, head_dim, hps, stream):
    if stream is None:
        qaug_ref, kaug_ref, v_tb_ref, o_ref, s0_ref, s1_ref, m_ref, acc_ref = refs
    else:
        (pt_ref, qaug_ref, kaug_ref, v_tb_ref, qb_ref, ck_hbm, o_ref, sc_ref,
         s0_ref, s1_ref, m_ref, acc_ref, buf_ref, sem_ref) = refs
        step = (pl.program_id(0) * pl.num_programs(1) + pl.program_id(1)) * pl.num_programs(2) \
            + pl.program_id(2)
        total = pl.num_programs(0) * pl.num_programs(1) * pl.num_programs(2)
        _stream_page_scores(step, total, stream["nc"], pt_ref, qb_ref, ck_hbm, sc_ref, buf_ref, sem_ref,
                            ch=stream["ch"], n_heads=stream["n_heads"], head_dim=head_dim, unrolled=True)
    j = pl.program_id(2)
    blk = MOBA_BLOCK
    key_i = lax.broadcasted_iota(jnp.int32, (blk, blk), 0)
    qry_i = lax.broadcasted_iota(jnp.int32, (blk, blk), 1)

    def scores(n, s_ref, own):
        for hh in range(hps):
            k_blk = kaug_ref[0, hh, pl.ds(pl.multiple_of(n * blk, blk), blk), :]
            s = _dot(k_blk, qaug_ref[0, hh])
            if own:
                s = jnp.where(key_i <= qry_i, s, NEG_INF)
            s_ref[hh] = s

    def probs(s_ref):
        out = []
        for hh in range(hps):
            s = s_ref[hh]
            m = m_ref[hh]
            m_new = jnp.maximum(m, jnp.max(s, axis=0, keepdims=True))
            m_ref[hh] = m_new
            out.append((jnp.exp2(m - m_new), jnp.exp2(s - m_new).astype(BF16)))
        return out

    def values(n, alpha_p):
        for hh, (alpha, p) in enumerate(alpha_p):
            pv = _dot(v_tb_ref[0, n, hh * V_ROWS:(hh + 1) * V_ROWS, :], p)
            acc_ref[hh] = alpha * acc_ref[hh] + pv

    def accumulate(s_ref, n):
        values(n, probs(s_ref))

    m_ref[...] = jnp.full(m_ref.shape, BELOW_NEG_INF, F32)
    acc_ref[...] = jnp.zeros(acc_ref.shape, F32)
    scores(j, s0_ref, own=True)

    bufs = (s0_ref, s1_ref)
    unroll = ATTN_BLOCKS_PER_ITER

    def run(first, count, pending):
        for t in range(count):
            scores(first + t, bufs[(t + 1) % 2], own=False)
            accumulate(bufs[t % 2], pending)
            pending = first + t
        return pending

    def group(i, carry):
        run(unroll * i, unroll, jnp.where(i == 0, j, unroll * i - 1))
        return carry

    n_groups = j // unroll
    lax.fori_loop(0, n_groups, group, 0)
    base = n_groups * unroll
    for rem in range(unroll):
        @pl.when(j - base == rem)
        def _(rem=rem):
            pending = run(base, rem, jnp.where(base == 0, j, base - 1))
            accumulate(bufs[rem % 2], pending)

    o_t = jnp.concatenate(
        [acc_ref[hh, 0:head_dim, :] / acc_ref[hh, head_dim:head_dim + 1, :] for hh in range(hps)], axis=0)
    o_ref[0] = o_t.T.astype(o_ref.dtype)


def _attn_grid(n_seq, seq_len, n_heads):
    return (n_seq, n_heads // ATTN_HEADS_PER_STEP, seq_len // MOBA_BLOCK)


def _moba_prompt_call(qaug, kaug, v_tb, *, n_heads, head_dim, stream=None):
    n_seq, _, seq_len, hw = kaug.shape
    n_blocks = seq_len // MOBA_BLOCK
    blk = MOBA_BLOCK
    hps = ATTN_HEADS_PER_STEP
    assert n_heads % hps == 0 and (hps * head_dim) % 128 == 0 and ATTN_BLOCKS_PER_ITER % 2 == 0
    grid = _attn_grid(n_seq, seq_len, n_heads)
    resident = pl.Buffered(1)
    in_specs = [
        pl.BlockSpec((1, hps, hw, blk), lambda b, hp, j, *_: (b, hp, 0, j)),
        pl.BlockSpec((1, hps, seq_len, hw), lambda b, hp, j, *_: (b, hp, 0, 0), pipeline_mode=resident),
        pl.BlockSpec((1, n_blocks, hps * V_ROWS, blk), lambda b, hp, j, *_: (b, 0, hp, 0),
                     pipeline_mode=resident),
    ]
    out_specs = pl.BlockSpec((1, blk, hps * head_dim), lambda b, hp, j, *_: (b, j, hp))
    out_shape = jax.ShapeDtypeStruct((n_seq, seq_len, n_heads * head_dim), BF16)
    scratch = [pltpu.VMEM((hps, blk, blk), F32),
               pltpu.VMEM((hps, blk, blk), F32),
               pltpu.VMEM((hps, 1, blk), F32),
               pltpu.VMEM((hps, V_ROWS, blk), F32)]
    params = pltpu.CompilerParams(dimension_semantics=("arbitrary", "arbitrary", "arbitrary"),
                                  vmem_limit_bytes=VMEM_LIMIT_BYTES)
    if stream is None:
        return pl.pallas_call(
            functools.partial(_moba_prompt_kernel, head_dim=head_dim, hps=hps, stream=None),
            grid=grid, in_specs=in_specs, out_specs=out_specs, out_shape=out_shape,
            scratch_shapes=scratch, compiler_params=params, name="moba_prompt",
        )(qaug, kaug, v_tb)

    page_table, qb, ck, ch = stream
    dec_seq, d_attn, page_size = qb.shape
    n_pages = page_table.shape[1]
    nc = n_pages // ch
    step_of = lambda b, hp, j: (b * grid[1] + hp) * grid[2] + j
    in_specs += [pl.BlockSpec((1, d_attn, page_size), lambda b, hp, j, pt: (step_of(b, hp, j) // nc, 0, 0)),
                 pl.BlockSpec(memory_space=pl.ANY)]
    sc_spec = pl.BlockSpec((1, ch, n_heads, page_size),
                           lambda b, hp, j, pt: (step_of(b, hp, j) // nc, step_of(b, hp, j) % nc, 0, 0))
    sc_shape = jax.ShapeDtypeStruct((dec_seq, n_pages, n_heads, page_size), F32)
    scratch += [pltpu.VMEM((2, ch, n_heads, head_dim, page_size), F32), pltpu.SemaphoreType.DMA((2,))]
    return pl.pallas_call(
        functools.partial(_moba_prompt_kernel, head_dim=head_dim, hps=hps,
                          stream=dict(nc=nc, ch=ch, n_heads=n_heads)),
        grid_spec=pltpu.PrefetchScalarGridSpec(
            num_scalar_prefetch=1, grid=grid, in_specs=in_specs,
            out_specs=[out_specs, sc_spec], scratch_shapes=scratch),
        out_shape=[out_shape, sc_shape], compiler_params=params, name="moba_prompt_stream",
    )(page_table, qaug, kaug, v_tb, qb, ck)


def _sample_scores_kernel(pt_ref, qb_ref, ck_hbm, sc_ref, buf_ref, sem_ref, *, ch, n_heads, head_dim):
    nc = pl.num_programs(1)
    step = pl.program_id(0) * nc + pl.program_id(1)
    total = pl.num_programs(0) * nc
    _stream_page_scores(step, total, nc, pt_ref, qb_ref, ck_hbm, sc_ref, buf_ref, sem_ref,
                        ch=ch, n_heads=n_heads, head_dim=head_dim, unrolled=False)


def _sample_probs_kernel(sc_ref, sself_ref, slopes_ref, p_ref, pself_ref, sel_ref,
                         *, n_pages, page_size, past_len, head_dim, n_heads):
    pages_per_block = MOBA_BLOCK // page_size
    n_blocks = n_pages // pages_per_block
    lanes = sself_ref.shape[-1]
    lane_i = lax.broadcasted_iota(jnp.int32, (n_heads, lanes), 1)
    scale = head_dim ** -0.5
    slope = slopes_ref[...]
    pos_in_page = lax.broadcasted_iota(jnp.int32, (n_heads, page_size), 1)

    def one_sequence(bb):
        gate = jnp.full((n_heads, lanes), NEG_INF, F32)
        for n in range(n_blocks):
            tot = sc_ref[bb, n * pages_per_block]
            for pg in range(1, pages_per_block):
                tot = tot + sc_ref[bb, n * pages_per_block + pg]
            g_n = jnp.sum(tot, axis=1, keepdims=True) * (1.0 / MOBA_BLOCK)
            gate = jnp.where(lane_i == n, g_n, gate)
        picks = []
        for _ in range(MOBA_TOPK):
            mx = jnp.max(gate, axis=1, keepdims=True)
            idx = jnp.min(jnp.where(gate == mx, lane_i, lanes), axis=1, keepdims=True)
            picks.append(idx)
            gate = jnp.where(lane_i == idx, BELOW_NEG_INF, gate)
        sel_vec = jnp.zeros((n_heads, lanes), jnp.int32)
        for t, idx in enumerate(picks):
            sel_vec = jnp.where(lane_i == t, idx, sel_vec)
        sel_ref[bb] = sel_vec

        s_self = sself_ref[bb] * scale

        def page_scores(pg):
            selected = picks[0] == (pg // pages_per_block)
            for idx in picks[1:]:
                selected = selected | (idx == (pg // pages_per_block))
            dist = (past_len - pg * page_size - pos_in_page).astype(F32)
            s = sc_ref[bb, pg] * scale - slope[:, 0:page_size] * dist
            return jnp.where(selected, s, NEG_INF)

        top = page_scores(0)
        for pg in range(1, n_pages):
            top = jnp.maximum(top, page_scores(pg))
        m = jnp.maximum(s_self, jnp.max(top, axis=1, keepdims=True))
        e_self = jnp.exp(s_self - m)
        tot = jnp.zeros((n_heads, page_size), F32)
        for pg in range(n_pages):
            e = jnp.exp(page_scores(pg) - m[:, 0:page_size])
            p_ref[bb, pg] = e
            tot = tot + e
        inv = 1.0 / (e_self + jnp.sum(tot, axis=1, keepdims=True))
        for pg in range(n_pages):
            p_ref[bb, pg] = p_ref[bb, pg] * inv[:, 0:page_size]
        pself_ref[bb] = e_self * inv

    for bb in range(sc_ref.shape[0]):
        one_sequence(bb)


def _sample_pv_kernel(pt_ref, sel_ref, p_ref, pself_e_ref, v_t_ref, cv_hbm, o_ref, vbuf_ref, sem_ref,
                      *, n_heads, head_dim, pages_per_block):
    b = pl.program_id(0)
    nb = pl.num_programs(0)
    per_head = MOBA_TOPK * pages_per_block

    def tile_copy(seq, h, t, pg, slot):
        blk = sel_ref[seq, h * MOBA_TOPK + t]
        page = pt_ref[seq, blk * pages_per_block + pg]
        k = h * per_head + t * pages_per_block + pg
        return pltpu.make_async_copy(cv_hbm.at[page, h], vbuf_ref.at[slot, k], sem_ref.at[slot])

    def for_all_tiles(seq, slot, fn):
        for h in range(n_heads):
            for t in range(MOBA_TOPK):
                for pg in range(pages_per_block):
                    fn(tile_copy(seq, h, t, pg, slot))

    @pl.when(b == 0)
    def _():
        for_all_tiles(b, 0, lambda cp: cp.start())
        o_ref[...] = jnp.zeros(o_ref.shape, F32)

    @pl.when(b + 1 < nb)
    def _():
        for_all_tiles(b + 1, (b + 1) % 2, lambda cp: cp.start())

    slot = b % 2
    for_all_tiles(b, slot, lambda cp: cp.wait())

    cols = []
    for h in range(n_heads):
        acc = jnp.zeros((head_dim, vbuf_ref.shape[-1]), F32)
        for t in range(MOBA_TOPK):
            blk = sel_ref[b, h * MOBA_TOPK + t]
            for pg in range(pages_per_block):
                k = h * per_head + t * pages_per_block + pg
                p_row = p_ref[0, blk * pages_per_block + pg, h:h + 1, :]
                acc = acc + vbuf_ref[slot, k] * p_row
        cols.append(jnp.sum(acc, axis=1, keepdims=True))
    col = jnp.concatenate(cols, axis=0)
    seq_i = lax.broadcasted_iota(jnp.int32, o_ref.shape, 1)
    o_ref[...] = jnp.where(seq_i == b, col, o_ref[...])

    @pl.when(b == nb - 1)
    def _():
        o_ref[...] = o_ref[...] + pself_e_ref[...] * v_t_ref[...]


def _cache_view(cache):
    return jnp.transpose(cache, (0, 2, 3, 1))


def _stream_chunk(n_steps, dec_seq, n_pages):
    total_pages = dec_seq * n_pages
    if total_pages % n_steps:
        return None
    ch = total_pages // n_steps
    return ch if ch <= PAGES_PER_CHUNK and n_pages % ch == 0 else None


def _sample_scores_call(q_t, ck, page_table, *, n_heads, head_dim):
    d_attn, n_seq = q_t.shape
    page_size = ck.shape[-1]
    n_pages = page_table.shape[1]
    ch = min(PAGES_PER_CHUNK, n_pages)
    assert n_pages % ch == 0
    qb = jnp.broadcast_to(q_t.T[:, :, None], (n_seq, d_attn, page_size))
    return pl.pallas_call(
        functools.partial(_sample_scores_kernel, ch=ch, n_heads=n_heads, head_dim=head_dim),
        grid_spec=pltpu.PrefetchScalarGridSpec(
            num_scalar_prefetch=1,
            grid=(n_seq, n_pages // ch),
            in_specs=[pl.BlockSpec((1, d_attn, page_size), lambda b, c, pt: (b, 0, 0)),
                      pl.BlockSpec(memory_space=pl.ANY)],
            out_specs=pl.BlockSpec((1, ch, n_heads, page_size), lambda b, c, pt: (b, c, 0, 0)),
            scratch_shapes=[pltpu.VMEM((2, ch, n_heads, head_dim, page_size), F32),
                            pltpu.SemaphoreType.DMA((2,))]),
        out_shape=jax.ShapeDtypeStruct((n_seq, n_pages, n_heads, page_size), F32),
        compiler_params=pltpu.CompilerParams(dimension_semantics=("arbitrary", "arbitrary"),
                                             vmem_limit_bytes=VMEM_LIMIT_BYTES),
        name="sample_scores",
    )(page_table, qb, ck)


def _moba_sample(scores, v_t, sself, slopes, cv, page_table, *, n_heads, head_dim):
    d_attn, n_seq = v_t.shape
    page_size = cv.shape[-1]
    n_pages = page_table.shape[1]
    past_len = n_pages * page_size
    pages_per_block = MOBA_BLOCK // page_size
    assert MOBA_BLOCK % page_size == 0 and past_len % MOBA_BLOCK == 0
    assert past_len // MOBA_BLOCK >= MOBA_TOPK

    lanes = 128
    sb = SAMPLE_SEQS_PER_STEP if n_seq % SAMPLE_SEQS_PER_STEP == 0 else 1
    sself_b = jnp.broadcast_to(sself.T[:, :, None], (n_seq, n_heads, lanes))
    slopes_b = jnp.broadcast_to(slopes[:, None], (n_heads, lanes))
    probs, pself, sel = pl.pallas_call(
        functools.partial(_sample_probs_kernel, n_pages=n_pages, page_size=page_size,
                          past_len=past_len, head_dim=head_dim, n_heads=n_heads),
        grid=(n_seq // sb,),
        in_specs=[pl.BlockSpec((sb, n_pages, n_heads, page_size), lambda b: (b, 0, 0, 0)),
                  pl.BlockSpec((sb, n_heads, lanes), lambda b: (b, 0, 0)),
                  pl.BlockSpec((n_heads, lanes), lambda b: (0, 0))],
        out_specs=[pl.BlockSpec((sb, n_pages, n_heads, page_size), lambda b: (b, 0, 0, 0)),
                   pl.BlockSpec((sb, n_heads, lanes), lambda b: (b, 0, 0)),
                   pl.BlockSpec((sb, n_heads, lanes), lambda b: (b, 0, 0))],
        out_shape=[jax.ShapeDtypeStruct((n_seq, n_pages, n_heads, page_size), F32),
                   jax.ShapeDtypeStruct((n_seq, n_heads, lanes), F32),
                   jax.ShapeDtypeStruct((n_seq, n_heads, lanes), jnp.int32)],
        compiler_params=pltpu.CompilerParams(dimension_semantics=("arbitrary",)),
        name="sample_probs",
    )(scores, sself_b, slopes_b)

    sel_flat = sel[:, :, :MOBA_TOPK].reshape(n_seq, n_heads * MOBA_TOPK)
    pself_e = jnp.repeat(pself[:, :, 0].T, head_dim, axis=0)
    n_tiles = n_heads * MOBA_TOPK * pages_per_block
    return pl.pallas_call(
        functools.partial(_sample_pv_kernel, n_heads=n_heads, head_dim=head_dim,
                          pages_per_block=pages_per_block),
        grid_spec=pltpu.PrefetchScalarGridSpec(
            num_scalar_prefetch=2,
            grid=(n_seq,),
            in_specs=[pl.BlockSpec((1, n_pages, n_heads, page_size), lambda b, pt, sl: (b, 0, 0, 0)),
                      pl.BlockSpec((d_attn, n_seq), lambda b, pt, sl: (0, 0)),
                      pl.BlockSpec((d_attn, n_seq), lambda b, pt, sl: (0, 0)),
                      pl.BlockSpec(memory_space=pl.ANY)],
            out_specs=pl.BlockSpec((d_attn, n_seq), lambda b, pt, sl: (0, 0)),
            scratch_shapes=[pltpu.VMEM((2, n_tiles, head_dim, page_size), F32),
                            pltpu.SemaphoreType.DMA((2,))]),
        out_shape=jax.ShapeDtypeStruct((d_attn, n_seq), F32),
        compiler_params=pltpu.CompilerParams(dimension_semantics=("arbitrary",)),
        name="sample_pv",
    )(page_table, sel_flat, probs, pself_e, v_t, cv)


def _post_kernel(attn_ref, siga_ref, gconv_ref, x_ref, p_ref, wao_ref, wo_ref, n2_ref, wg_ref, wu_ref,
                 wd_ref, n3_ref, wple_ref, wpg_ref, y_ref, *, ff_chunks):
    y_attn = _dot(attn_ref[...], wao_ref[...])
    merged = siga_ref[...].astype(F32) * y_attn + gconv_ref[...].astype(F32)
    x1 = x_ref[...] + _dot(merged.astype(BF16), wo_ref[...])
    h2 = _rms_norm(x1, n2_ref[...]).astype(BF16)
    x2 = x1
    for lo, hi in ff_chunks:
        g = _dot(h2, wg_ref[:, lo:hi])
        u = _dot(h2, wu_ref[:, lo:hi])
        a = (g * jax.nn.sigmoid(g) * u).astype(BF16)
        x2 = x2 + _dot(a, wd_ref[lo:hi, :])
    h3 = _rms_norm(x2, n3_ref[...]).astype(BF16)
    ple = _dot(p_ref[...].astype(BF16), wple_ref[...])
    y_ref[...] = x2 + ple * jax.nn.sigmoid(_dot(h3, wpg_ref[...]))


def _post_call(attn, siga, gconv, x2d, p2d, weights, *, tm):
    (wao, wo, n2, wg, wu, wd, n3, wple, wpg) = weights
    rows, d_model = x2d.shape
    d_ff = wg.shape[1]
    chunk = 1024
    ff_chunks = tuple((lo, min(lo + chunk, d_ff)) for lo in range(0, d_ff, chunk))
    row = lambda w: pl.BlockSpec((tm, w), lambda i: (i, 0))
    return pl.pallas_call(
        functools.partial(_post_kernel, ff_chunks=ff_chunks),
        grid=(rows // tm,),
        in_specs=[row(attn.shape[1]), row(d_model), row(d_model), row(d_model), row(p2d.shape[1])]
                 + [_const_spec(w.shape) for w in weights],
        out_specs=row(d_model),
        out_shape=jax.ShapeDtypeStruct((rows, d_model), F32),
        compiler_params=pltpu.CompilerParams(dimension_semantics=("arbitrary",),
                                             vmem_limit_bytes=VMEM_LIMIT_BYTES),
        name="post",
    )(attn, siga, gconv, x2d, p2d, *weights)


def kernel(x_prompt, x_sample, p_prompt, p_sample, cache_k, cache_v, state_conv, page_table, norm1, w_in, q_norm, k_norm, conv_w, w_attn_out, w_conv_out, w_o, norm2, w_gate, w_up, w_down, norm3, w_ple, w_ple_gate):
    depth = w_in.shape[0]
    n_seq, seq_len, d_model = x_prompt.shape
    dec_seq, dec_len, _ = x_sample.shape
    head_dim = q_norm.shape[-1]
    d_attn = w_attn_out.shape[1]
    d_conv = w_conv_out.shape[1]
    n_heads = d_attn // head_dim
    n_blocks = seq_len // MOBA_BLOCK
    assert dec_len == 1, "sample path handles one new token per sequence"
    assert conv_w.shape[1] == CONV_WIDTH
    assert seq_len % ROW_TILE == 0 and ROW_TILE % MOBA_BLOCK == 0
    assert MOBA_BLOCK == 1 << MOBA_BLOCK_LOG2
    assert n_blocks <= AUG_MAX_BLOCKS and AUG_END <= head_dim and head_dim < V_ROWS
    slopes = 2.0 ** (-8.0 * jnp.arange(1, n_heads + 1, dtype=F32) / n_heads)

    yp = x_prompt
    ys = x_sample.reshape(dec_seq, d_model)
    outs = [[] for _ in range(6)]
    for i in range(depth):
        proj_w = (norm1[i][None, :],
                  w_in[i][:, :3 * d_attn].T.astype(BF16),
                  w_in[i][:, 3 * d_attn:].astype(BF16),
                  jnp.tile(q_norm[i], n_heads)[:, None],
                  jnp.tile(k_norm[i], n_heads)[:, None],
                  conv_w[i], w_conv_out[i].astype(BF16), n_heads, head_dim)
        post_w = (w_attn_out[i].astype(BF16), w_o[i].astype(BF16), norm2[i][None, :],
                  w_gate[i].astype(BF16), w_up[i].astype(BF16), w_down[i].astype(BF16),
                  norm3[i][None, :], w_ple[i].astype(BF16), w_ple_gate[i].astype(BF16))

        hist = state_conv[i]
        sq_t, sk_t, sv_t, sself, sgconv, ssiga, su = _proj_call(
            ys[None], (hist[:, 0], hist[:, 1]), proj_w, seq_mode=False, tm=dec_seq)
        ck = _cache_view(cache_k[i])
        n_pages = page_table.shape[1]
        attn_grid = _attn_grid(n_seq, seq_len, n_heads)
        ch = _stream_chunk(attn_grid[0] * attn_grid[1] * attn_grid[2], dec_seq, n_pages)

        q_t, k_t, v_t, v_tb, kaug, kbar, gconv, siga, tails = _proj_call(
            yp, None, proj_w, seq_mode=True, tm=ROW_TILE)
        kbar_h = jnp.transpose(kbar[..., :head_dim], (0, 2, 1, 3, 4)).reshape(
            n_seq, n_heads, n_blocks, head_dim)
        kbar_h = jnp.pad(kbar_h, ((0, 0), (0, 0), (0, AUG_MAX_BLOCKS - n_blocks), (0, 0)))
        qaug = _select_call(slopes, q_t, kbar_h, n_heads=n_heads, head_dim=head_dim)
        if ch is None:
            attn = _moba_prompt_call(qaug, kaug, v_tb, n_heads=n_heads, head_dim=head_dim)
            scores = _sample_scores_call(sq_t[0], ck, page_table, n_heads=n_heads, head_dim=head_dim)
        else:
            qb = jnp.broadcast_to(sq_t[0].T[:, :, None], (dec_seq, d_attn, ck.shape[-1]))
            attn, scores = _moba_prompt_call(qaug, kaug, v_tb, n_heads=n_heads, head_dim=head_dim,
                                             stream=(page_table, qb, ck, ch))
        yp = _post_call(attn.reshape(n_seq * seq_len, d_attn), siga, gconv,
                        yp.reshape(n_seq * seq_len, d_model),
                        p_prompt[i].reshape(n_seq * seq_len, -1), post_w,
                        tm=ROW_TILE).reshape(n_seq, seq_len, d_model)
        to_cache_layout = lambda t, n, l: jnp.transpose(
            t.reshape(n, n_heads, head_dim, l), (0, 3, 1, 2))
        outs[0].append(to_cache_layout(k_t, n_seq, seq_len))
        outs[1].append(to_cache_layout(v_t, n_seq, seq_len))
        tiles_per_seq = seq_len // ROW_TILE
        outs[2].append(tails.reshape(n_seq, tiles_per_seq, CONV_WIDTH - 1, d_conv)[:, -1])

        o_t = _moba_sample(scores, sv_t[0], sself, slopes, _cache_view(cache_v[i]), page_table,
                           n_heads=n_heads, head_dim=head_dim)
        ys = _post_call(o_t.T.astype(BF16), ssiga, sgconv, ys, p_sample[i].reshape(dec_seq, -1),
                        post_w, tm=dec_seq)
        outs[3].append(to_cache_layout(sk_t, 1, dec_seq).reshape(dec_seq, 1, n_heads, head_dim))
        outs[4].append(to_cache_layout(sv_t, 1, dec_seq).reshape(dec_seq, 1, n_heads, head_dim))
        outs[5].append(jnp.stack([hist[:, 1], su], axis=1))

    return (yp, ys.reshape(dec_seq, 1, d_model), *[jnp.stack(o) for o in outs])
```

```python
import functools

import jax
import jax.numpy as jnp
from jax import lax
from jax.experimental import pallas as pl
from jax.experimental.pallas import tpu as pltpu

RMS_EPS = 1e-6
NEG_INF = -1e30
BELOW_NEG_INF = -3e38
MOBA_BLOCK = 256
MOBA_BLOCK_LOG2 = 8
MOBA_TOPK = 3
CONV_WIDTH = 3
LOG2E = 1.4426950408889634
AUG_MAX_BLOCKS = 32
SPLIT = 3
AUG_KBLK = AUG_MAX_BLOCKS
AUG_KOFF = AUG_KBLK + SPLIT
AUG_ONE = AUG_KOFF + SPLIT
AUG_END = AUG_ONE + SPLIT
V_ROWS = 80
ATTN_HEADS_PER_STEP = 8
ATTN_BLOCKS_PER_ITER = 4
SUBLANES = 8
VMEM_LIMIT_BYTES = 60 * 1024 * 1024
ROW_TILE = 512
SAMPLE_SEQS_PER_STEP = 4
PAGES_PER_CHUNK = 32

F32 = jnp.float32
BF16 = jnp.bfloat16


def _dot(a, b):
    return jnp.dot(a, b, preferred_element_type=F32)


def _dot_nt(a, b):
    return lax.dot_general(a, b, (((1,), (1,)), ((), ())), preferred_element_type=F32)


def _split_bf16(a):
    hi = a.astype(BF16)
    lo = (a - hi.astype(F32)).astype(BF16)
    return hi, lo


def _dot_3pass(a, b):
    ah, al = _split_bf16(a)
    bh, bl = _split_bf16(b)
    return _dot(ah, bh) + (_dot(ah, bl) + _dot(al, bh))


def _rms_norm(x, w):
    ms = jnp.mean(x * x, axis=-1, keepdims=True)
    return x * lax.rsqrt(ms + RMS_EPS) * w


def _head_norm_t(z_t, w_col, n_heads, head_dim):
    tokens = z_t.shape[1]
    z3 = z_t.reshape(n_heads, head_dim, tokens)
    ms = jnp.mean(z3 * z3, axis=1, keepdims=True)
    y = (z3 * lax.rsqrt(ms + RMS_EPS)).reshape(n_heads * head_dim, tokens)
    return y * w_col


def _const_spec(shape):
    zeros = (0,) * len(shape)
    return pl.BlockSpec(shape, lambda *_: zeros, pipeline_mode=pl.Buffered(1))


def _key_aug_rows(pos0, tokens, rows):
    r = lax.broadcasted_iota(jnp.int32, (rows, tokens), 0)
    pos = pos0 + lax.broadcasted_iota(jnp.int32, (rows, tokens), 1)
    block = jnp.right_shift(pos, MOBA_BLOCK_LOG2)
    offset = jnp.bitwise_and(pos, MOBA_BLOCK - 1).astype(F32)
    one = (r == block) | ((r >= AUG_ONE) & (r < AUG_END))
    is_blk = (r >= AUG_KBLK) & (r < AUG_KOFF)
    is_off = (r >= AUG_KOFF) & (r < AUG_ONE)
    return jnp.where(one, 1.0, jnp.where(is_blk, block.astype(F32), jnp.where(is_off, offset, 0.0)))


def _split3_bf16(a):
    p0 = a.astype(BF16).astype(F32)
    p1 = (a - p0).astype(BF16).astype(F32)
    p2 = (a - p0 - p1).astype(BF16).astype(F32)
    return p0, p1, p2


def _proj_kernel(*refs, seq_mode, tm, n_heads, head_dim, d_attn, d_conv, d_model):
    if seq_mode:
        (slopes_ref, x_ref, n1_ref, wqkv_t_ref, wrest_ref, qn_ref, kn_ref, convw_ref, wco_ref,
         k_t_ref, v_t_ref, v_tb_ref, kaug_ref, qaug_ref, gconv_ref, siga_ref, tail_ref,
         ubuf_ref, kbar_ref) = refs
    else:
        (x_ref, h0_ref, h1_ref, n1_ref, wqkv_t_ref, wrest_ref, qn_ref, kn_ref, convw_ref, wco_ref,
         q_t_ref, k_t_ref, v_t_ref, sself_ref, gconv_ref, siga_ref, u_ref) = refs

    x = x_ref[0]
    h = _rms_norm(x, n1_ref[...]).astype(BF16)

    q_t = _head_norm_t(_dot_nt(wqkv_t_ref[0:d_attn, :], h), qn_ref[...], n_heads, head_dim)
    k_t = _head_norm_t(_dot_nt(wqkv_t_ref[d_attn:2 * d_attn, :], h), kn_ref[...], n_heads, head_dim)
    v_t = _dot_nt(wqkv_t_ref[2 * d_attn:3 * d_attn, :], h)
    k_t_ref[0] = k_t
    v_t_ref[0] = v_t

    if seq_mode:
        tile = pl.program_id(1)
        nb_t = tm // MOBA_BLOCK

        @pl.when(tile == 0)
        def _():
            kbar_ref[...] = jnp.zeros(kbar_ref.shape, F32)
        aug_t = _key_aug_rows(tile * tm, tm, head_dim)
        for hd in range(n_heads):
            k_h = jnp.concatenate([k_t[hd * head_dim:(hd + 1) * head_dim], aug_t], axis=0).T
            kaug_ref[0, hd] = k_h.astype(BF16)
            for i in range(nb_t):
                kbar_ref[hd, pl.ds(tile * nb_t + i, 1), :] = jnp.mean(
                    k_h[i * MOBA_BLOCK:(i + 1) * MOBA_BLOCK], axis=0, keepdims=True)
            qaug_ref[0, hd] = _widened_queries(q_t[hd * head_dim:(hd + 1) * head_dim],
                                               kbar_ref[hd][:, 0:head_dim], slopes_ref[hd], tile * tm)
        pad_r = lax.broadcasted_iota(jnp.int32, (V_ROWS - head_dim, MOBA_BLOCK), 0)
        ones_pad = jnp.where(pad_r == 0, 1.0, 0.0)
        for i in range(tm // MOBA_BLOCK):
            lo = i * MOBA_BLOCK
            for hd in range(n_heads):
                v_h = v_t[hd * head_dim:(hd + 1) * head_dim, lo:lo + MOBA_BLOCK]
                v_tb_ref[0, i, hd * V_ROWS:(hd + 1) * V_ROWS, :] = jnp.concatenate(
                    [v_h, ones_pad], axis=0).astype(BF16)
    else:
        q_t_ref[0] = q_t
        qk = (q_t * k_t).reshape(n_heads, head_dim, tm)
        sself_ref[...] = jnp.sum(qk, axis=1)

    zc = _dot(h, wrest_ref[:, 0:3 * d_conv])
    cb = zc[:, 0:d_conv]
    u = zc[:, d_conv:2 * d_conv] * zc[:, 2 * d_conv:3 * d_conv]
    if seq_mode:
        @pl.when(pl.program_id(1) == 0)
        def _():
            ubuf_ref[0:SUBLANES, :] = jnp.zeros((SUBLANES, d_conv), F32)
        ubuf_ref[SUBLANES:SUBLANES + tm, :] = u
        u_m2 = ubuf_ref[SUBLANES - 2:SUBLANES - 2 + tm, :]
        u_m1 = ubuf_ref[SUBLANES - 1:SUBLANES - 1 + tm, :]
        tail_ref[0] = u[tm - (CONV_WIDTH - 1):tm, :]
        ubuf_ref[0:SUBLANES, :] = u[tm - SUBLANES:tm, :]
    else:
        u_m2 = h0_ref[...]
        u_m1 = h1_ref[...]
        u_ref[...] = u
    conv = convw_ref[0:1, :] * u_m2 + convw_ref[1:2, :] * u_m1 + convw_ref[2:3, :] * u
    y_conv = _dot((cb * conv).astype(BF16), wco_ref[...])

    zg = _dot(h, wrest_ref[:, 3 * d_conv:3 * d_conv + 2 * d_model])
    siga_ref[...] = jax.nn.sigmoid(zg[:, 0:d_model]).astype(BF16)
    gconv_ref[...] = (jax.nn.sigmoid(zg[:, d_model:2 * d_model]) * y_conv).astype(BF16)


def _proj_call(x3, hist, weights, *, seq_mode, tm, slopes=None):
    (n1, wqkv_t, wrest, qn_col, kn_col, convw, wco, n_heads, head_dim) = weights
    n_seq, seq_len, d_model = x3.shape
    d_attn = n_heads * head_dim
    d_conv = convw.shape[1]
    n_tiles = seq_len // tm
    rows = n_seq * seq_len
    kern = functools.partial(_proj_kernel, seq_mode=seq_mode, tm=tm, n_heads=n_heads,
                             head_dim=head_dim, d_attn=d_attn, d_conv=d_conv, d_model=d_model)
    row_map = lambda b, t: (b * n_tiles + t, 0)
    w_specs = [_const_spec(n1.shape), _const_spec(wqkv_t.shape), _const_spec(wrest.shape),
               _const_spec(qn_col.shape), _const_spec(kn_col.shape), _const_spec(convw.shape),
               _const_spec(wco.shape)]
    t_spec = pl.BlockSpec((1, d_attn, tm), lambda b, t: (b, 0, t))
    t_shape = jax.ShapeDtypeStruct((n_seq, d_attn, seq_len), F32)
    gate_specs = [pl.BlockSpec((tm, d_model), row_map)] * 2
    gate_shapes = [jax.ShapeDtypeStruct((rows, d_model), BF16)] * 2
    x_spec = pl.BlockSpec((1, tm, d_model), lambda b, t: (b, t, 0))
    if seq_mode:
        nb_t = tm // MOBA_BLOCK
        in_specs = [pl.BlockSpec(memory_space=pltpu.SMEM), x_spec] + w_specs
        args = (slopes, x3)
        out_specs = [t_spec, t_spec,
                     pl.BlockSpec((1, nb_t, n_heads * V_ROWS, MOBA_BLOCK), lambda b, t: (b, t, 0, 0)),
                     pl.BlockSpec((1, n_heads, tm, 2 * head_dim), lambda b, t: (b, 0, t, 0)),
                     pl.BlockSpec((1, n_heads, 2 * head_dim, tm), lambda b, t: (b, 0, 0, t)),
                     *gate_specs,
                     pl.BlockSpec((1, CONV_WIDTH - 1, d_conv), lambda b, t: (b * n_tiles + t, 0, 0))]
        out_shape = [t_shape, t_shape,
                     jax.ShapeDtypeStruct((n_seq, seq_len // MOBA_BLOCK, n_heads * V_ROWS, MOBA_BLOCK),
                                          BF16),
                     jax.ShapeDtypeStruct((n_seq, n_heads, seq_len, 2 * head_dim), BF16),
                     jax.ShapeDtypeStruct((n_seq, n_heads, 2 * head_dim, seq_len), BF16),
                     *gate_shapes,
                     jax.ShapeDtypeStruct((n_seq * n_tiles, CONV_WIDTH - 1, d_conv), F32)]
        scratch = [pltpu.VMEM((SUBLANES + tm, d_conv), F32),
                   pltpu.VMEM((n_heads, AUG_MAX_BLOCKS, 2 * head_dim), F32)]
    else:
        h_spec = pl.BlockSpec((tm, d_conv), row_map)
        in_specs = [x_spec, h_spec, h_spec] + w_specs
        args = (x3, hist[0], hist[1])
        out_specs = [t_spec, t_spec, t_spec,
                     pl.BlockSpec((n_heads, tm), lambda b, t: (0, b * n_tiles + t)),
                     *gate_specs,
                     pl.BlockSpec((tm, d_conv), row_map)]
        out_shape = [t_shape, t_shape, t_shape,
                     jax.ShapeDtypeStruct((n_heads, rows), F32),
                     *gate_shapes,
                     jax.ShapeDtypeStruct((rows, d_conv), F32)]
        scratch = []
    return pl.pallas_call(
        kern, grid=(n_seq, n_tiles), in_specs=in_specs, out_specs=out_specs, out_shape=out_shape,
        scratch_shapes=scratch,
        compiler_params=pltpu.CompilerParams(dimension_semantics=("arbitrary", "arbitrary"),
                                             vmem_limit_bytes=VMEM_LIMIT_BYTES),
        name="proj_seq" if seq_mode else "proj_rows",
    )(*args, n1, wqkv_t, wrest, qn_col, kn_col, convw, wco)


def _topk_rows(gate, row_iota, k):
    n = gate.shape[0]
    picks = []
    for _ in range(k):
        mx = jnp.max(gate, axis=0, keepdims=True)
        idx = jnp.min(jnp.where(gate == mx, row_iota, n), axis=0, keepdims=True)
        picks.append(idx)
        gate = jnp.where(row_iota == idx, BELOW_NEG_INF, gate)
    return picks


def _widened_queries(q, kbar, slope, pos0):
    head_dim, tq = q.shape
    gate = _dot_3pass(kbar, q)
    blk_i = lax.broadcasted_iota(jnp.int32, (AUG_MAX_BLOCKS, tq), 0)
    pos = pos0 + lax.broadcasted_iota(jnp.int32, (AUG_MAX_BLOCKS, tq), 1)
    q_blk = jnp.right_shift(pos, MOBA_BLOCK_LOG2)
    past = blk_i < q_blk
    gate = jnp.where(past, gate, NEG_INF)
    mask = jnp.where(blk_i == q_blk, 0.0, NEG_INF)
    for idx in _topk_rows(gate, blk_i, MOBA_TOPK):
        mask = jnp.where((blk_i == idx) & past, 0.0, mask)
    tail_rows = head_dim - AUG_MAX_BLOCKS
    r = AUG_MAX_BLOCKS + lax.broadcasted_iota(jnp.int32, (tail_rows, tq), 0)
    q_pos = (pos0 + lax.broadcasted_iota(jnp.int32, (tail_rows, tq), 1)).astype(F32)
    c = jnp.full((tail_rows, tq), slope * LOG2E, F32)
    c_parts = _split3_bf16(c)
    v_parts = _split3_bf16(-(c * q_pos))
    tail = jnp.zeros((tail_rows, tq), F32)
    for i in range(SPLIT):
        tail = jnp.where(r == AUG_KBLK + i, MOBA_BLOCK * c_parts[i], tail)
        tail = jnp.where(r == AUG_KOFF + i, c_parts[i], tail)
        tail = jnp.where(r == AUG_ONE + i, v_parts[i], tail)
    scale = head_dim ** -0.5 * LOG2E
    return jnp.concatenate([q * scale, mask, tail], axis=0).astype(BF16)


def _stream_page_scores(step, total, nc, pt_ref, qb_ref, ck_hbm, sc_ref, buf_ref, sem_ref,
                        *, ch, n_heads, head_dim):
    def page_copy(at_step, i):
        page = pt_ref[at_step // nc, (at_step % nc) * ch + i]
        slot = at_step % 2
        return pltpu.make_async_copy(ck_hbm.at[page], buf_ref.at[slot, i], sem_ref.at[slot])

    @pl.when(step == 0)
    def _():
        for i in range(ch):
            page_copy(step, i).start()

    @pl.when(step + 1 < total)
    def _():
        for i in range(ch):
            page_copy(step + 1, i).start()

    for i in range(ch):
        page_copy(step, i).wait()
    slot = step % 2

    def score_page(i):
        rows = []
        for h in range(n_heads):
            prod = buf_ref[slot, i, h] * qb_ref[0, h * head_dim:(h + 1) * head_dim, :]
            rows.append(jnp.sum(prod, axis=0, keepdims=True))
        sc_ref[0, i] = jnp.concatenate(rows, axis=0)

    return score_page


def _moba_prompt_kernel(*refs, head_dim, hps, stream):
    if stream is None:
        qaug_ref, kaug_ref, v_tb_ref, o_ref, s0_ref, s1_ref, m_ref, acc_ref = refs
    else:
        (pt_ref, qaug_ref, kaug_ref, v_tb_ref, qb_ref, ck_hbm, o_ref, sc_ref,
         s0_ref, s1_ref, m_ref, acc_ref, buf_ref, sem_ref) = refs
        step = (pl.program_id(0) * pl.num_programs(1) + pl.program_id(1)) * pl.num_programs(2) \
            + pl.program_id(2)
        total = pl.num_programs(0) * pl.num_programs(1) * pl.num_programs(2)
        score_page = _stream_page_scores(step, total, stream["nc"], pt_ref, qb_ref, ck_hbm, sc_ref,
                                         buf_ref, sem_ref, ch=stream["ch"], n_heads=stream["n_heads"],
                                         head_dim=head_dim)
        for i in range(stream["ch"]):
            score_page(i)
    j = pl.program_id(2)
    blk = MOBA_BLOCK
    key_i = lax.broadcasted_iota(jnp.int32, (blk, blk), 0)
    qry_i = lax.broadcasted_iota(jnp.int32, (blk, blk), 1)

    def scores(n, s_ref, own):
        for hh in range(hps):
            k_blk = kaug_ref[0, hh, pl.ds(pl.multiple_of(n * blk, blk), blk), :]
            s = _dot(k_blk, qaug_ref[0, hh])
            if own:
                s = jnp.where(key_i <= qry_i, s, NEG_INF)
            s_ref[hh] = s

    def probs(s_ref):
        out = []
        for hh in range(hps):
            s = s_ref[hh]
            m = m_ref[hh]
            m_new = jnp.maximum(m, jnp.max(s, axis=0, keepdims=True))
            m_ref[hh] = m_new
            out.append((jnp.exp2(m - m_new), jnp.exp2(s - m_new).astype(BF16)))
        return out

    def values(n, alpha_p):
        for hh, (alpha, p) in enumerate(alpha_p):
            pv = _dot(v_tb_ref[0, n, hh * V_ROWS:(hh + 1) * V_ROWS, :], p)
            acc_ref[hh] = alpha * acc_ref[hh] + pv

    def accumulate(s_ref, n):
        values(n, probs(s_ref))

    m_ref[...] = jnp.full(m_ref.shape, BELOW_NEG_INF, F32)
    acc_ref[...] = jnp.zeros(acc_ref.shape, F32)
    scores(j, s0_ref, own=True)

    bufs = (s0_ref, s1_ref)
    unroll = ATTN_BLOCKS_PER_ITER

    def run(first, count, pending):
        for t in range(count):
            scores(first + t, bufs[(t + 1) % 2], own=False)
            accumulate(bufs[t % 2], pending)
            pending = first + t
        return pending

    def group(i, carry):
        run(unroll * i, unroll, jnp.where(i == 0, j, unroll * i - 1))
        return carry

    n_groups = j // unroll
    lax.fori_loop(0, n_groups, group, 0)
    base = n_groups * unroll
    for rem in range(unroll):
        @pl.when(j - base == rem)
        def _(rem=rem):
            pending = run(base, rem, jnp.where(base == 0, j, base - 1))
            accumulate(bufs[rem % 2], pending)

    o_t = jnp.concatenate(
        [acc_ref[hh, 0:head_dim, :] / acc_ref[hh, head_dim:head_dim + 1, :] for hh in range(hps)], axis=0)
    o_ref[0] = o_t.T.astype(o_ref.dtype)


def _attn_grid(n_seq, seq_len, n_heads):
    return (n_seq, n_heads // ATTN_HEADS_PER_STEP, seq_len // MOBA_BLOCK)


def _moba_prompt_call(qaug, kaug, v_tb, *, n_heads, head_dim, stream=None):
    n_seq, _, seq_len, hw = kaug.shape
    n_blocks = seq_len // MOBA_BLOCK
    blk = MOBA_BLOCK
    hps = ATTN_HEADS_PER_STEP
    assert n_heads % hps == 0 and (hps * head_dim) % 128 == 0 and ATTN_BLOCKS_PER_ITER % 2 == 0
    grid = _attn_grid(n_seq, seq_len, n_heads)
    resident = pl.Buffered(1)
    in_specs = [
        pl.BlockSpec((1, hps, hw, blk), lambda b, hp, j, *_: (b, hp, 0, j)),
        pl.BlockSpec((1, hps, seq_len, hw), lambda b, hp, j, *_: (b, hp, 0, 0), pipeline_mode=resident),
        pl.BlockSpec((1, n_blocks, hps * V_ROWS, blk), lambda b, hp, j, *_: (b, 0, hp, 0),
                     pipeline_mode=resident),
    ]
    out_specs = pl.BlockSpec((1, blk, hps * head_dim), lambda b, hp, j, *_: (b, j, hp))
    out_shape = jax.ShapeDtypeStruct((n_seq, seq_len, n_heads * head_dim), BF16)
    scratch = [pltpu.VMEM((hps, blk, blk), F32),
               pltpu.VMEM((hps, blk, blk), F32),
               pltpu.VMEM((hps, 1, blk), F32),
               pltpu.VMEM((hps, V_ROWS, blk), F32)]
    params = pltpu.CompilerParams(dimension_semantics=("arbitrary", "arbitrary", "arbitrary"),
                                  vmem_limit_bytes=VMEM_LIMIT_BYTES)
    if stream is None:
        return pl.pallas_call(
            functools.partial(_moba_prompt_kernel, head_dim=head_dim, hps=hps, stream=None),
            grid=grid, in_specs=in_specs, out_specs=out_specs, out_shape=out_shape,
            scratch_shapes=scratch, compiler_params=params, name="moba_prompt",
        )(qaug, kaug, v_tb)

    page_table, qb, ck, ch = stream
    dec_seq, d_attn, page_size = qb.shape
    n_pages = page_table.shape[1]
    nc = n_pages // ch
    step_of = lambda b, hp, j: (b * grid[1] + hp) * grid[2] + j
    in_specs += [pl.BlockSpec((1, d_attn, page_size), lambda b, hp, j, pt: (step_of(b, hp, j) // nc, 0, 0)),
                 pl.BlockSpec(memory_space=pl.ANY)]
    sc_spec = pl.BlockSpec((1, ch, n_heads, page_size),
                           lambda b, hp, j, pt: (step_of(b, hp, j) // nc, step_of(b, hp, j) % nc, 0, 0))
    sc_shape = jax.ShapeDtypeStruct((dec_seq, n_pages, n_heads, page_size), F32)
    scratch += [pltpu.VMEM((2, ch, n_heads, head_dim, page_size), F32), pltpu.SemaphoreType.DMA((2,))]
    return pl.pallas_call(
        functools.partial(_moba_prompt_kernel, head_dim=head_dim, hps=hps,
                          stream=dict(nc=nc, ch=ch, n_heads=n_heads)),
        grid_spec=pltpu.PrefetchScalarGridSpec(
            num_scalar_prefetch=1, grid=grid, in_specs=in_specs,
            out_specs=[out_specs, sc_spec], scratch_shapes=scratch),
        out_shape=[out_shape, sc_shape], compiler_params=params, name="moba_prompt_stream",
    )(page_table, qaug, kaug, v_tb, qb, ck)


def _sample_scores_kernel(pt_ref, qb_ref, ck_hbm, sc_ref, buf_ref, sem_ref, *, ch, n_heads, head_dim):
    nc = pl.num_programs(1)
    step = pl.program_id(0) * nc + pl.program_id(1)
    total = pl.num_programs(0) * nc
    score_page = _stream_page_scores(step, total, nc, pt_ref, qb_ref, ck_hbm, sc_ref, buf_ref, sem_ref,
                                     ch=ch, n_heads=n_heads, head_dim=head_dim)

    def body(i, carry):
        score_page(i)
        return carry

    lax.fori_loop(0, ch, body, 0)


def _sample_probs_kernel(sc_ref, sself_ref, slopes_ref, p_ref, pself_ref, sel_ref,
                         *, n_pages, page_size, past_len, head_dim, n_heads):
    pages_per_block = MOBA_BLOCK // page_size
    n_blocks = n_pages // pages_per_block
    lanes = sself_ref.shape[-1]
    lane_i = lax.broadcasted_iota(jnp.int32, (n_heads, lanes), 1)
    scale = head_dim ** -0.5
    slope = slopes_ref[...]
    pos_in_page = lax.broadcasted_iota(jnp.int32, (n_heads, page_size), 1)

    def one_sequence(bb):
        gate = jnp.full((n_heads, lanes), NEG_INF, F32)
        for n in range(n_blocks):
            tot = sc_ref[bb, n * pages_per_block]
            for pg in range(1, pages_per_block):
                tot = tot + sc_ref[bb, n * pages_per_block + pg]
            g_n = jnp.sum(tot, axis=1, keepdims=True) * (1.0 / MOBA_BLOCK)
            gate = jnp.where(lane_i == n, g_n, gate)
        picks = []
        for _ in range(MOBA_TOPK):
            mx = jnp.max(gate, axis=1, keepdims=True)
            idx = jnp.min(jnp.where(gate == mx, lane_i, lanes), axis=1, keepdims=True)
            picks.append(idx)
            gate = jnp.where(lane_i == idx, BELOW_NEG_INF, gate)
        sel_vec = jnp.zeros((n_heads, lanes), jnp.int32)
        for t, idx in enumerate(picks):
            sel_vec = jnp.where(lane_i == t, idx, sel_vec)
        sel_ref[bb] = sel_vec

        s_self = sself_ref[bb] * scale

        def page_scores(pg):
            selected = picks[0] == (pg // pages_per_block)
            for idx in picks[1:]:
                selected = selected | (idx == (pg // pages_per_block))
            dist = (past_len - pg * page_size - pos_in_page).astype(F32)
            s = sc_ref[bb, pg] * scale - slope[:, 0:page_size] * dist
            return jnp.where(selected, s, NEG_INF)

        top = page_scores(0)
        for pg in range(1, n_pages):
            top = jnp.maximum(top, page_scores(pg))
        m = jnp.maximum(s_self, jnp.max(top, axis=1, keepdims=True))
        e_self = jnp.exp(s_self - m)
        tot = jnp.zeros((n_heads, page_size), F32)
        for pg in range(n_pages):
            e = jnp.exp(page_scores(pg) - m[:, 0:page_size])
            p_ref[bb, pg] = e
            tot = tot + e
        inv = 1.0 / (e_self + jnp.sum(tot, axis=1, keepdims=True))
        for pg in range(n_pages):
            p_ref[bb, pg] = p_ref[bb, pg] * inv[:, 0:page_size]
        pself_ref[bb] = e_self * inv

    for bb in range(sc_ref.shape[0]):
        one_sequence(bb)


def _sample_pv_kernel(pt_ref, sel_ref, p_ref, pself_e_ref, v_t_ref, cv_hbm, o_ref, vbuf_ref, sem_ref,
                      *, n_heads, head_dim, pages_per_block):
    b = pl.program_id(0)
    nb = pl.num_programs(0)
    per_head = MOBA_TOPK * pages_per_block

    def tile_copy(seq, h, t, pg, slot):
        blk = sel_ref[seq, h * MOBA_TOPK + t]
        page = pt_ref[seq, blk * pages_per_block + pg]
        k = h * per_head + t * pages_per_block + pg
        return pltpu.make_async_copy(cv_hbm.at[page, h], vbuf_ref.at[slot, k], sem_ref.at[slot])

    def for_all_tiles(seq, slot, fn):
        for h in range(n_heads):
            for t in range(MOBA_TOPK):
                for pg in range(pages_per_block):
                    fn(tile_copy(seq, h, t, pg, slot))

    @pl.when(b == 0)
    def _():
        for_all_tiles(b, 0, lambda cp: cp.start())
        o_ref[...] = jnp.zeros(o_ref.shape, F32)

    @pl.when(b + 1 < nb)
    def _():
        for_all_tiles(b + 1, (b + 1) % 2, lambda cp: cp.start())

    slot = b % 2
    for_all_tiles(b, slot, lambda cp: cp.wait())

    cols = []
    for h in range(n_heads):
        acc = jnp.zeros((head_dim, vbuf_ref.shape[-1]), F32)
        for t in range(MOBA_TOPK):
            blk = sel_ref[b, h * MOBA_TOPK + t]
            for pg in range(pages_per_block):
                k = h * per_head + t * pages_per_block + pg
                p_row = p_ref[0, blk * pages_per_block + pg, h:h + 1, :]
                acc = acc + vbuf_ref[slot, k] * p_row
        cols.append(jnp.sum(acc, axis=1, keepdims=True))
    col = jnp.concatenate(cols, axis=0)
    seq_i = lax.broadcasted_iota(jnp.int32, o_ref.shape, 1)
    o_ref[...] = jnp.where(seq_i == b, col, o_ref[...])

    @pl.when(b == nb - 1)
    def _():
        o_ref[...] = o_ref[...] + pself_e_ref[...] * v_t_ref[...]


def _cache_view(cache):
    return jnp.transpose(cache, (0, 2, 3, 1))


def _stream_chunk(n_steps, dec_seq, n_pages):
    total_pages = dec_seq * n_pages
    if total_pages % n_steps:
        return None
    ch = total_pages // n_steps
    return ch if ch <= PAGES_PER_CHUNK and n_pages % ch == 0 else None


def _sample_scores_call(q_t, ck, page_table, *, n_heads, head_dim):
    d_attn, n_seq = q_t.shape
    page_size = ck.shape[-1]
    n_pages = page_table.shape[1]
    ch = min(PAGES_PER_CHUNK, n_pages)
    assert n_pages % ch == 0
    qb = jnp.broadcast_to(q_t.T[:, :, None], (n_seq, d_attn, page_size))
    return pl.pallas_call(
        functools.partial(_sample_scores_kernel, ch=ch, n_heads=n_heads, head_dim=head_dim),
        grid_spec=pltpu.PrefetchScalarGridSpec(
            num_scalar_prefetch=1,
            grid=(n_seq, n_pages // ch),
            in_specs=[pl.BlockSpec((1, d_attn, page_size), lambda b, c, pt: (b, 0, 0)),
                      pl.BlockSpec(memory_space=pl.ANY)],
            out_specs=pl.BlockSpec((1, ch, n_heads, page_size), lambda b, c, pt: (b, c, 0, 0)),
            scratch_shapes=[pltpu.VMEM((2, ch, n_heads, head_dim, page_size), F32),
                            pltpu.SemaphoreType.DMA((2,))]),
        out_shape=jax.ShapeDtypeStruct((n_seq, n_pages, n_heads, page_size), F32),
        compiler_params=pltpu.CompilerParams(dimension_semantics=("arbitrary", "arbitrary"),
                                             vmem_limit_bytes=VMEM_LIMIT_BYTES),
        name="sample_scores",
    )(page_table, qb, ck)


def _moba_sample(scores, v_t, sself, slopes, cv, page_table, *, n_heads, head_dim):
    d_attn, n_seq = v_t.shape
    page_size = cv.shape[-1]
    n_pages = page_table.shape[1]
    past_len = n_pages * page_size
    pages_per_block = MOBA_BLOCK // page_size
    assert MOBA_BLOCK % page_size == 0 and past_len % MOBA_BLOCK == 0
    assert past_len // MOBA_BLOCK >= MOBA_TOPK

    lanes = 128
    sb = SAMPLE_SEQS_PER_STEP if n_seq % SAMPLE_SEQS_PER_STEP == 0 else 1
    sself_b = jnp.broadcast_to(sself.T[:, :, None], (n_seq, n_heads, lanes))
    slopes_b = jnp.broadcast_to(slopes[:, None], (n_heads, lanes))
    probs, pself, sel = pl.pallas_call(
        functools.partial(_sample_probs_kernel, n_pages=n_pages, page_size=page_size,
                          past_len=past_len, head_dim=head_dim, n_heads=n_heads),
        grid=(n_seq // sb,),
        in_specs=[pl.BlockSpec((sb, n_pages, n_heads, page_size), lambda b: (b, 0, 0, 0)),
                  pl.BlockSpec((sb, n_heads, lanes), lambda b: (b, 0, 0)),
                  pl.BlockSpec((n_heads, lanes), lambda b: (0, 0))],
        out_specs=[pl.BlockSpec((sb, n_pages, n_heads, page_size), lambda b: (b, 0, 0, 0)),
                   pl.BlockSpec((sb, n_heads, lanes), lambda b: (b, 0, 0)),
                   pl.BlockSpec((sb, n_heads, lanes), lambda b: (b, 0, 0))],
        out_shape=[jax.ShapeDtypeStruct((n_seq, n_pages, n_heads, page_size), F32),
                   jax.ShapeDtypeStruct((n_seq, n_heads, lanes), F32),
                   jax.ShapeDtypeStruct((n_seq, n_heads, lanes), jnp.int32)],
        compiler_params=pltpu.CompilerParams(dimension_semantics=("arbitrary",)),
        name="sample_probs",
    )(scores, sself_b, slopes_b)

    sel_flat = sel[:, :, :MOBA_TOPK].reshape(n_seq, n_heads * MOBA_TOPK)
    pself_e = jnp.repeat(pself[:, :, 0].T, head_dim, axis=0)
    n_tiles = n_heads * MOBA_TOPK * pages_per_block
    return pl.pallas_call(
        functools.partial(_sample_pv_kernel, n_heads=n_heads, head_dim=head_dim,
                          pages_per_block=pages_per_block),
        grid_spec=pltpu.PrefetchScalarGridSpec(
            num_scalar_prefetch=2,
            grid=(n_seq,),
            in_specs=[pl.BlockSpec((1, n_pages, n_heads, page_size), lambda b, pt, sl: (b, 0, 0, 0)),
                      pl.BlockSpec((d_attn, n_seq), lambda b, pt, sl: (0, 0)),
                      pl.BlockSpec((d_attn, n_seq), lambda b, pt, sl: (0, 0)),
                      pl.BlockSpec(memory_space=pl.ANY)],
            out_specs=pl.BlockSpec((d_attn, n_seq), lambda b, pt, sl: (0, 0)),
            scratch_shapes=[pltpu.VMEM((2, n_tiles, head_dim, page_size), F32),
                            pltpu.SemaphoreType.DMA((2,))]),
        out_shape=jax.ShapeDtypeStruct((d_attn, n_seq), F32),
        compiler_params=pltpu.CompilerParams(dimension_semantics=("arbitrary",)),
        name="sample_pv",
    )(page_table, sel_flat, probs, pself_e, v_t, cv)


def _post_kernel(attn_ref, siga_ref, gconv_ref, x_ref, p_ref, wao_ref, wo_ref, n2_ref, wg_ref, wu_ref,
                 wd_ref, n3_ref, wple_ref, wpg_ref, y_ref, *, ff_chunks):
    y_attn = _dot(attn_ref[...], wao_ref[...])
    merged = siga_ref[...].astype(F32) * y_attn + gconv_ref[...].astype(F32)
    x1 = x_ref[...] + _dot(merged.astype(BF16), wo_ref[...])
    h2 = _rms_norm(x1, n2_ref[...]).astype(BF16)
    x2 = x1
    for lo, hi in ff_chunks:
        g = _dot(h2, wg_ref[:, lo:hi])
        u = _dot(h2, wu_ref[:, lo:hi])
        a = (g * jax.nn.sigmoid(g) * u).astype(BF16)
        x2 = x2 + _dot(a, wd_ref[lo:hi, :])
    h3 = _rms_norm(x2, n3_ref[...]).astype(BF16)
    ple = _dot(p_ref[...].astype(BF16), wple_ref[...])
    y_ref[...] = x2 + ple * jax.nn.sigmoid(_dot(h3, wpg_ref[...]))


def _post_call(attn, siga, gconv, x2d, p2d, weights, *, tm):
    (wao, wo, n2, wg, wu, wd, n3, wple, wpg) = weights
    rows, d_model = x2d.shape
    d_ff = wg.shape[1]
    chunk = 1024
    ff_chunks = tuple((lo, min(lo + chunk, d_ff)) for lo in range(0, d_ff, chunk))
    row = lambda w: pl.BlockSpec((tm, w), lambda i: (i, 0))
    return pl.pallas_call(
        functools.partial(_post_kernel, ff_chunks=ff_chunks),
        grid=(rows // tm,),
        in_specs=[row(attn.shape[1]), row(d_model), row(d_model), row(d_model), row(p2d.shape[1])]
                 + [_const_spec(w.shape) for w in weights],
        out_specs=row(d_model),
        out_shape=jax.ShapeDtypeStruct((rows, d_model), F32),
        compiler_params=pltpu.CompilerParams(dimension_semantics=("arbitrary",),
                                             vmem_limit_bytes=VMEM_LIMIT_BYTES),
        name="post",
    )(attn, siga, gconv, x2d, p2d, *weights)


def kernel(x_prompt, x_sample, p_prompt, p_sample, cache_k, cache_v, state_conv, page_table, norm1, w_in, q_norm, k_norm, conv_w, w_attn_out, w_conv_out, w_o, norm2, w_gate, w_up, w_down, norm3, w_ple, w_ple_gate):
    depth = w_in.shape[0]
    n_seq, seq_len, d_model = x_prompt.shape
    dec_seq, dec_len, _ = x_sample.shape
    head_dim = q_norm.shape[-1]
    d_attn = w_attn_out.shape[1]
    d_conv = w_conv_out.shape[1]
    n_heads = d_attn // head_dim
    n_blocks = seq_len // MOBA_BLOCK
    assert dec_len == 1, "sample path handles one new token per sequence"
    assert conv_w.shape[1] == CONV_WIDTH
    assert seq_len % ROW_TILE == 0 and ROW_TILE % MOBA_BLOCK == 0
    assert MOBA_BLOCK == 1 << MOBA_BLOCK_LOG2
    assert n_blocks <= AUG_MAX_BLOCKS and AUG_END <= head_dim and head_dim < V_ROWS
    slopes = 2.0 ** (-8.0 * jnp.arange(1, n_heads + 1, dtype=F32) / n_heads)

    yp = x_prompt
    ys = x_sample.reshape(dec_seq, d_model)
    outs = [[] for _ in range(6)]
    for i in range(depth):
        proj_w = (norm1[i][None, :],
                  w_in[i][:, :3 * d_attn].T.astype(BF16),
                  w_in[i][:, 3 * d_attn:].astype(BF16),
                  jnp.tile(q_norm[i], n_heads)[:, None],
                  jnp.tile(k_norm[i], n_heads)[:, None],
                  conv_w[i], w_conv_out[i].astype(BF16), n_heads, head_dim)
        post_w = (w_attn_out[i].astype(BF16), w_o[i].astype(BF16), norm2[i][None, :],
                  w_gate[i].astype(BF16), w_up[i].astype(BF16), w_down[i].astype(BF16),
                  norm3[i][None, :], w_ple[i].astype(BF16), w_ple_gate[i].astype(BF16))

        hist = state_conv[i]
        sq_t, sk_t, sv_t, sself, sgconv, ssiga, su = _proj_call(
            ys[None], (hist[:, 0], hist[:, 1]), proj_w, seq_mode=False, tm=dec_seq)
        ck = _cache_view(cache_k[i])
        n_pages = page_table.shape[1]
        attn_grid = _attn_grid(n_seq, seq_len, n_heads)
        ch = _stream_chunk(attn_grid[0] * attn_grid[1] * attn_grid[2], dec_seq, n_pages)

        k_t, v_t, v_tb, kaug, qaug, gconv, siga, tails = _proj_call(
            yp, None, proj_w, seq_mode=True, tm=ROW_TILE, slopes=slopes)
        if ch is None:
            attn = _moba_prompt_call(qaug, kaug, v_tb, n_heads=n_heads, head_dim=head_dim)
            scores = _sample_scores_call(sq_t[0], ck, page_table, n_heads=n_heads, head_dim=head_dim)
        else:
            qb = jnp.broadcast_to(sq_t[0].T[:, :, None], (dec_seq, d_attn, ck.shape[-1]))
            attn, scores = _moba_prompt_call(qaug, kaug, v_tb, n_heads=n_heads, head_dim=head_dim,
                                             stream=(page_table, qb, ck, ch))
        yp = _post_call(attn.reshape(n_seq * seq_len, d_attn), siga, gconv,
                        yp.reshape(n_seq * seq_len, d_model),
                        p_prompt[i].reshape(n_seq * seq_len, -1), post_w,
                        tm=ROW_TILE).reshape(n_seq, seq_len, d_model)
        to_cache_layout = lambda t, n, l: jnp.transpose(
            t.reshape(n, n_heads, head_dim, l), (0, 3, 1, 2))
        outs[0].append(to_cache_layout(k_t, n_seq, seq_len))
        outs[1].append(to_cache_layout(v_t, n_seq, seq_len))
        tiles_per_seq = seq_len // ROW_TILE
        outs[2].append(tails.reshape(n_seq, tiles_per_seq, CONV_WIDTH - 1, d_conv)[:, -1])

        o_t = _moba_sample(scores, sv_t[0], sself, slopes, _cache_view(cache_v[i]), page_table,
                           n_heads=n_heads, head_dim=head_dim)
        ys = _post_call(o_t.T.astype(BF16), ssiga, sgconv, ys, p_sample[i].reshape(dec_seq, -1),
                        post_w, tm=dec_seq)
        outs[3].append(to_cache_layout(sk_t, 1, dec_seq).reshape(dec_seq, 1, n_heads, head_dim))
        outs[4].append(to_cache_layout(sv_t, 1, dec_seq).reshape(dec_seq, 1, n_heads, head_dim))
        outs[5].append(jnp.stack([hist[:, 1], su], axis=1))

    return (yp, ys.reshape(dec_seq, 1, d_model), *[jnp.stack(o) for o in outs])
```

```python
import functools

import jax
import jax.numpy as jnp
from jax import lax
from jax.experimental import pallas as pl
from jax.experimental.pallas import tpu as pltpu

RMS_EPS = 1e-6
NEG_INF = -1e30
BELOW_NEG_INF = -3e38
MOBA_BLOCK = 256
MOBA_BLOCK_LOG2 = 8
MOBA_TOPK = 3
CONV_WIDTH = 3
LOG2E = 1.4426950408889634
AUG_MAX_BLOCKS = 32
SPLIT = 3
AUG_KBLK = AUG_MAX_BLOCKS
AUG_KOFF = AUG_KBLK + SPLIT
AUG_ONE = AUG_KOFF + SPLIT
AUG_END = AUG_ONE + SPLIT
V_ROWS = 80
ATTN_HEADS_PER_STEP = 8
ATTN_BLOCKS_PER_ITER = 4
SUBLANES = 8
VMEM_LIMIT_BYTES = 60 * 1024 * 1024
ROW_TILE = 512
SAMPLE_SEQS_PER_STEP = 4
PAGES_PER_CHUNK = 32

F32 = jnp.float32
BF16 = jnp.bfloat16


def _dot(a, b):
    return jnp.dot(a, b, preferred_element_type=F32)


def _dot_nt(a, b):
    return lax.dot_general(a, b, (((1,), (1,)), ((), ())), preferred_element_type=F32)


def _split_bf16(a):
    hi = a.astype(BF16)
    lo = (a - hi.astype(F32)).astype(BF16)
    return hi, lo


def _dot_3pass(a, b):
    ah, al = _split_bf16(a)
    bh, bl = _split_bf16(b)
    return _dot(ah, bh) + (_dot(ah, bl) + _dot(al, bh))


def _rms_norm(x, w):
    ms = jnp.mean(x * x, axis=-1, keepdims=True)
    return x * lax.rsqrt(ms + RMS_EPS) * w


def _head_norm_t(z_t, w_col, n_heads, head_dim):
    tokens = z_t.shape[1]
    z3 = z_t.reshape(n_heads, head_dim, tokens)
    ms = jnp.mean(z3 * z3, axis=1, keepdims=True)
    y = (z3 * lax.rsqrt(ms + RMS_EPS)).reshape(n_heads * head_dim, tokens)
    return y * w_col


def _const_spec(shape):
    zeros = (0,) * len(shape)
    return pl.BlockSpec(shape, lambda *_: zeros, pipeline_mode=pl.Buffered(1))


def _key_aug_rows(pos0, tokens, rows):
    r = lax.broadcasted_iota(jnp.int32, (rows, tokens), 0)
    pos = pos0 + lax.broadcasted_iota(jnp.int32, (rows, tokens), 1)
    block = jnp.right_shift(pos, MOBA_BLOCK_LOG2)
    offset = jnp.bitwise_and(pos, MOBA_BLOCK - 1).astype(F32)
    one = (r == block) | ((r >= AUG_ONE) & (r < AUG_END))
    is_blk = (r >= AUG_KBLK) & (r < AUG_KOFF)
    is_off = (r >= AUG_KOFF) & (r < AUG_ONE)
    return jnp.where(one, 1.0, jnp.where(is_blk, block.astype(F32), jnp.where(is_off, offset, 0.0)))


def _split3_bf16(a):
    p0 = a.astype(BF16).astype(F32)
    p1 = (a - p0).astype(BF16).astype(F32)
    p2 = (a - p0 - p1).astype(BF16).astype(F32)
    return p0, p1, p2


def _proj_kernel(*refs, seq_mode, tm, n_heads, head_dim, d_attn, d_conv, d_model):
    if seq_mode:
        (slopes_ref, x_ref, n1_ref, wqkv_t_ref, wrest_ref, qn_ref, kn_ref, convw_ref, wco_ref,
         k_t_ref, v_t_ref, v_tb_ref, kaug_ref, qaug_ref, gconv_ref, siga_ref, tail_ref,
         ubuf_ref, kbar_ref) = refs
    else:
        (x_ref, h0_ref, h1_ref, n1_ref, wqkv_t_ref, wrest_ref, qn_ref, kn_ref, convw_ref, wco_ref,
         q_t_ref, k_t_ref, v_t_ref, sself_ref, gconv_ref, siga_ref, u_ref) = refs

    x = x_ref[0]
    h = _rms_norm(x, n1_ref[...]).astype(BF16)

    q_t = _head_norm_t(_dot_nt(wqkv_t_ref[0:d_attn, :], h), qn_ref[...], n_heads, head_dim)
    k_t = _head_norm_t(_dot_nt(wqkv_t_ref[d_attn:2 * d_attn, :], h), kn_ref[...], n_heads, head_dim)
    v_t = _dot_nt(wqkv_t_ref[2 * d_attn:3 * d_attn, :], h)
    k_t_ref[0] = k_t
    v_t_ref[0] = v_t

    if seq_mode:
        tile = pl.program_id(1)
        nb_t = tm // MOBA_BLOCK

        @pl.when(tile == 0)
        def _():
            kbar_ref[...] = jnp.zeros(kbar_ref.shape, F32)
        aug_t = _key_aug_rows(tile * tm, tm, head_dim)
        for hd in range(n_heads):
            k_h = jnp.concatenate([k_t[hd * head_dim:(hd + 1) * head_dim], aug_t], axis=0).T
            kaug_ref[0, hd] = k_h.astype(BF16)
            for i in range(nb_t):
                kbar_ref[hd, pl.ds(tile * nb_t + i, 1), :] = jnp.mean(
                    k_h[i * MOBA_BLOCK:(i + 1) * MOBA_BLOCK], axis=0, keepdims=True)
            qaug_ref[0, hd] = _widened_queries(q_t[hd * head_dim:(hd + 1) * head_dim],
                                               kbar_ref[hd][:, 0:head_dim], slopes_ref[hd], tile * tm)
        pad_r = lax.broadcasted_iota(jnp.int32, (V_ROWS - head_dim, MOBA_BLOCK), 0)
        ones_pad = jnp.where(pad_r == 0, 1.0, 0.0)
        for i in range(tm // MOBA_BLOCK):
            lo = i * MOBA_BLOCK
            for hd in range(n_heads):
                v_h = v_t[hd * head_dim:(hd + 1) * head_dim, lo:lo + MOBA_BLOCK]
                v_tb_ref[0, i, hd * V_ROWS:(hd + 1) * V_ROWS, :] = jnp.concatenate(
                    [v_h, ones_pad], axis=0).astype(BF16)
    else:
        q_t_ref[0] = q_t
        qk = (q_t * k_t).reshape(n_heads, head_dim, tm)
        sself_ref[...] = jnp.sum(qk, axis=1)

    zc = _dot(h, wrest_ref[:, 0:3 * d_conv])
    cb = zc[:, 0:d_conv]
    u = zc[:, d_conv:2 * d_conv] * zc[:, 2 * d_conv:3 * d_conv]
    if seq_mode:
        @pl.when(pl.program_id(1) == 0)
        def _():
            ubuf_ref[0:SUBLANES, :] = jnp.zeros((SUBLANES, d_conv), F32)
        ubuf_ref[SUBLANES:SUBLANES + tm, :] = u
        u_m2 = ubuf_ref[SUBLANES - 2:SUBLANES - 2 + tm, :]
        u_m1 = ubuf_ref[SUBLANES - 1:SUBLANES - 1 + tm, :]
        tail_ref[0] = u[tm - (CONV_WIDTH - 1):tm, :]
        ubuf_ref[0:SUBLANES, :] = u[tm - SUBLANES:tm, :]
    else:
        u_m2 = h0_ref[...]
        u_m1 = h1_ref[...]
        u_ref[...] = u
    conv = convw_ref[0:1, :] * u_m2 + convw_ref[1:2, :] * u_m1 + convw_ref[2:3, :] * u
    y_conv = _dot((cb * conv).astype(BF16), wco_ref[...])

    zg = _dot(h, wrest_ref[:, 3 * d_conv:3 * d_conv + 2 * d_model])
    siga_ref[...] = jax.nn.sigmoid(zg[:, 0:d_model]).astype(BF16)
    gconv_ref[...] = (jax.nn.sigmoid(zg[:, d_model:2 * d_model]) * y_conv).astype(BF16)


def _proj_call(x3, hist, weights, *, seq_mode, tm, slopes=None):
    (n1, wqkv_t, wrest, qn_col, kn_col, convw, wco, n_heads, head_dim) = weights
    n_seq, seq_len, d_model = x3.shape
    d_attn = n_heads * head_dim
    d_conv = convw.shape[1]
    n_tiles = seq_len // tm
    rows = n_seq * seq_len
    kern = functools.partial(_proj_kernel, seq_mode=seq_mode, tm=tm, n_heads=n_heads,
                             head_dim=head_dim, d_attn=d_attn, d_conv=d_conv, d_model=d_model)
    row_map = lambda b, t: (b * n_tiles + t, 0)
    w_specs = [_const_spec(n1.shape), _const_spec(wqkv_t.shape), _const_spec(wrest.shape),
               _const_spec(qn_col.shape), _const_spec(kn_col.shape), _const_spec(convw.shape),
               _const_spec(wco.shape)]
    t_spec = pl.BlockSpec((1, d_attn, tm), lambda b, t: (b, 0, t))
    t_shape = jax.ShapeDtypeStruct((n_seq, d_attn, seq_len), F32)
    gate_specs = [pl.BlockSpec((tm, d_model), row_map)] * 2
    gate_shapes = [jax.ShapeDtypeStruct((rows, d_model), BF16)] * 2
    x_spec = pl.BlockSpec((1, tm, d_model), lambda b, t: (b, t, 0))
    if seq_mode:
        nb_t = tm // MOBA_BLOCK
        in_specs = [pl.BlockSpec(memory_space=pltpu.SMEM), x_spec] + w_specs
        args = (slopes, x3)
        out_specs = [t_spec, t_spec,
                     pl.BlockSpec((1, nb_t, n_heads * V_ROWS, MOBA_BLOCK), lambda b, t: (b, t, 0, 0)),
                     pl.BlockSpec((1, n_heads, tm, 2 * head_dim), lambda b, t: (b, 0, t, 0)),
                     pl.BlockSpec((1, n_heads, 2 * head_dim, tm), lambda b, t: (b, 0, 0, t)),
                     *gate_specs,
                     pl.BlockSpec((1, CONV_WIDTH - 1, d_conv), lambda b, t: (b * n_tiles + t, 0, 0))]
        out_shape = [t_shape, t_shape,
                     jax.ShapeDtypeStruct((n_seq, seq_len // MOBA_BLOCK, n_heads * V_ROWS, MOBA_BLOCK),
                                          BF16),
                     jax.ShapeDtypeStruct((n_seq, n_heads, seq_len, 2 * head_dim), BF16),
                     jax.ShapeDtypeStruct((n_seq, n_heads, 2 * head_dim, seq_len), BF16),
                     *gate_shapes,
                     jax.ShapeDtypeStruct((n_seq * n_tiles, CONV_WIDTH - 1, d_conv), F32)]
        scratch = [pltpu.VMEM((SUBLANES + tm, d_conv), F32),
                   pltpu.VMEM((n_heads, AUG_MAX_BLOCKS, 2 * head_dim), F32)]
    else:
        h_spec = pl.BlockSpec((tm, d_conv), row_map)
        in_specs = [x_spec, h_spec, h_spec] + w_specs
        args = (x3, hist[0], hist[1])
        out_specs = [t_spec, t_spec, t_spec,
                     pl.BlockSpec((n_heads, tm), lambda b, t: (0, b * n_tiles + t)),
                     *gate_specs,
                     pl.BlockSpec((tm, d_conv), row_map)]
        out_shape = [t_shape, t_shape, t_shape,
                     jax.ShapeDtypeStruct((n_heads, rows), F32),
                     *gate_shapes,
                     jax.ShapeDtypeStruct((rows, d_conv), F32)]
        scratch = []
    return pl.pallas_call(
        kern, grid=(n_seq, n_tiles), in_specs=in_specs, out_specs=out_specs, out_shape=out_shape,
        scratch_shapes=scratch,
        compiler_params=pltpu.CompilerParams(dimension_semantics=("arbitrary", "arbitrary"),
                                             vmem_limit_bytes=VMEM_LIMIT_BYTES),
        name="proj_seq" if seq_mode else "proj_rows",
    )(*args, n1, wqkv_t, wrest, qn_col, kn_col, convw, wco)


def _topk_rows(gate, row_iota, k):
    n = gate.shape[0]
    picks = []
    for _ in range(k):
        mx = jnp.max(gate, axis=0, keepdims=True)
        idx = jnp.min(jnp.where(gate == mx, row_iota, n), axis=0, keepdims=True)
        picks.append(idx)
        gate = jnp.where(row_iota == idx, BELOW_NEG_INF, gate)
    return picks


def _widened_queries(q, kbar, slope, pos0):
    head_dim, tq = q.shape
    gate = _dot_3pass(kbar, q)
    blk_i = lax.broadcasted_iota(jnp.int32, (AUG_MAX_BLOCKS, tq), 0)
    pos = pos0 + lax.broadcasted_iota(jnp.int32, (AUG_MAX_BLOCKS, tq), 1)
    q_blk = jnp.right_shift(pos, MOBA_BLOCK_LOG2)
    past = blk_i < q_blk
    gate = jnp.where(past, gate, NEG_INF)
    mask = jnp.where(blk_i == q_blk, 0.0, NEG_INF)
    for idx in _topk_rows(gate, blk_i, MOBA_TOPK):
        mask = jnp.where((blk_i == idx) & past, 0.0, mask)
    tail_rows = head_dim - AUG_MAX_BLOCKS
    r = AUG_MAX_BLOCKS + lax.broadcasted_iota(jnp.int32, (tail_rows, tq), 0)
    q_pos = (pos0 + lax.broadcasted_iota(jnp.int32, (tail_rows, tq), 1)).astype(F32)
    c = jnp.full((tail_rows, tq), slope * LOG2E, F32)
    c_parts = _split3_bf16(c)
    v_parts = _split3_bf16(-(c * q_pos))
    tail = jnp.zeros((tail_rows, tq), F32)
    for i in range(SPLIT):
        tail = jnp.where(r == AUG_KBLK + i, MOBA_BLOCK * c_parts[i], tail)
        tail = jnp.where(r == AUG_KOFF + i, c_parts[i], tail)
        tail = jnp.where(r == AUG_ONE + i, v_parts[i], tail)
    scale = head_dim ** -0.5 * LOG2E
    return jnp.concatenate([q * scale, mask, tail], axis=0).astype(BF16)


def _stream_page_scores(step, total, nc, pt_ref, qb_ref, ck_hbm, sc_ref, buf_ref, sem_ref,
                        *, ch, n_heads, head_dim):
    def page_copy(at_step, i):
        page = pt_ref[at_step // nc, (at_step % nc) * ch + i]
        slot = at_step % 2
        return pltpu.make_async_copy(ck_hbm.at[page], buf_ref.at[slot, i], sem_ref.at[slot])

    @pl.when(step == 0)
    def _():
        for i in range(ch):
            page_copy(step, i).start()

    @pl.when(step + 1 < total)
    def _():
        for i in range(ch):
            page_copy(step + 1, i).start()

    for i in range(ch):
        page_copy(step, i).wait()
    slot = step % 2

    def score_page(i):
        rows = []
        for h in range(n_heads):
            prod = buf_ref[slot, i, h] * qb_ref[0, h * head_dim:(h + 1) * head_dim, :]
            rows.append(jnp.sum(prod, axis=0, keepdims=True))
        sc_ref[0, i] = jnp.concatenate(rows, axis=0)

    return score_page


def _moba_prompt_kernel(*refs, head_dim, hps, stream):
    if stream is None:
        qaug_ref, kaug_ref, v_tb_ref, o_ref, s0_ref, s1_ref, m_ref, acc_ref = refs
    else:
        (pt_ref, qaug_ref, kaug_ref, v_tb_ref, qb_ref, ck_hbm, o_ref, sc_ref,
         s0_ref, s1_ref, m_ref, acc_ref, buf_ref, sem_ref) = refs
        step = (pl.program_id(0) * pl.num_programs(1) + pl.program_id(1)) * pl.num_programs(2) \
            + pl.program_id(2)
        total = pl.num_programs(0) * pl.num_programs(1) * pl.num_programs(2)
        score_page = _stream_page_scores(step, total, stream["nc"], pt_ref, qb_ref, ck_hbm, sc_ref,
                                         buf_ref, sem_ref, ch=stream["ch"], n_heads=stream["n_heads"],
                                         head_dim=head_dim)
        for i in range(stream["ch"]):
            score_page(i)
    j = pl.program_id(2)
    blk = MOBA_BLOCK
    key_i = lax.broadcasted_iota(jnp.int32, (blk, blk), 0)
    qry_i = lax.broadcasted_iota(jnp.int32, (blk, blk), 1)

    def scores(n, s_ref, own):
        for hh in range(hps):
            k_blk = kaug_ref[0, hh, pl.ds(pl.multiple_of(n * blk, blk), blk), :]
            s = _dot(k_blk, qaug_ref[0, hh])
            if own:
                s = jnp.where(key_i <= qry_i, s, NEG_INF)
            s_ref[hh] = s

    def probs(s_ref):
        out = []
        for hh in range(hps):
            s = s_ref[hh]
            m = m_ref[hh]
            m_new = jnp.maximum(m, jnp.max(s, axis=0, keepdims=True))
            m_ref[hh] = m_new
            out.append((jnp.exp2(m - m_new), jnp.exp2(s - m_new).astype(BF16)))
        return out

    def values(n, alpha_p):
        for hh, (alpha, p) in enumerate(alpha_p):
            pv = _dot(v_tb_ref[0, n, hh * V_ROWS:(hh + 1) * V_ROWS, :], p)
            acc_ref[hh] = alpha * acc_ref[hh] + pv

    def accumulate(s_ref, n):
        values(n, probs(s_ref))

    m_ref[...] = jnp.full(m_ref.shape, BELOW_NEG_INF, F32)
    acc_ref[...] = jnp.zeros(acc_ref.shape, F32)
    scores(j, s0_ref, own=True)

    bufs = (s0_ref, s1_ref)
    unroll = ATTN_BLOCKS_PER_ITER

    def run(first, count, pending):
        for t in range(count):
            scores(first + t, bufs[(t + 1) % 2], own=False)
            accumulate(bufs[t % 2], pending)
            pending = first + t
        return pending

    def group(i, carry):
        run(unroll * i, unroll, jnp.where(i == 0, j, unroll * i - 1))
        return carry

    n_groups = j // unroll
    lax.fori_loop(0, n_groups, group, 0)
    base = n_groups * unroll
    for rem in range(unroll):
        @pl.when(j - base == rem)
        def _(rem=rem):
            pending = run(base, rem, jnp.where(base == 0, j, base - 1))
            accumulate(bufs[rem % 2], pending)

    o_t = jnp.concatenate(
        [acc_ref[hh, 0:head_dim, :] / acc_ref[hh, head_dim:head_dim + 1, :] for hh in range(hps)], axis=0)
    o_ref[0] = o_t.T.astype(o_ref.dtype)


def _attn_grid(n_seq, seq_len, n_heads):
    return (n_seq, n_heads // ATTN_HEADS_PER_STEP, seq_len // MOBA_BLOCK)


def _moba_prompt_call(qaug, kaug, v_tb, *, n_heads, head_dim, stream=None):
    n_seq, _, seq_len, hw = kaug.shape
    n_blocks = seq_len // MOBA_BLOCK
    blk = MOBA_BLOCK
    hps = ATTN_HEADS_PER_STEP
    assert n_heads % hps == 0 and (hps * head_dim) % 128 == 0 and ATTN_BLOCKS_PER_ITER % 2 == 0
    grid = _attn_grid(n_seq, seq_len, n_heads)
    resident = pl.Buffered(1)
    in_specs = [
        pl.BlockSpec((1, hps, hw, blk), lambda b, hp, j, *_: (b, hp, 0, j)),
        pl.BlockSpec((1, hps, seq_len, hw), lambda b, hp, j, *_: (b, hp, 0, 0), pipeline_mode=resident),
        pl.BlockSpec((1, n_blocks, hps * V_ROWS, blk), lambda b, hp, j, *_: (b, 0, hp, 0),
                     pipeline_mode=resident),
    ]
    out_specs = pl.BlockSpec((1, blk, hps * head_dim), lambda b, hp, j, *_: (b, j, hp))
    out_shape = jax.ShapeDtypeStruct((n_seq, seq_len, n_heads * head_dim), BF16)
    scratch = [pltpu.VMEM((hps, blk, blk), F32),
               pltpu.VMEM((hps, blk, blk), F32),
               pltpu.VMEM((hps, 1, blk), F32),
               pltpu.VMEM((hps, V_ROWS, blk), F32)]
    params = pltpu.CompilerParams(dimension_semantics=("arbitrary", "arbitrary", "arbitrary"),
                                  vmem_limit_bytes=VMEM_LIMIT_BYTES)
    if stream is None:
        return pl.pallas_call(
            functools.partial(_moba_prompt_kernel, head_dim=head_dim, hps=hps, stream=None),
            grid=grid, in_specs=in_specs, out_specs=out_specs, out_shape=out_shape,
            scratch_shapes=scratch, compiler_params=params, name="moba_prompt",
        )(qaug, kaug, v_tb)

    page_table, qb, ck, ch = stream
    dec_seq, d_attn, page_size = qb.shape
    n_pages = page_table.shape[1]
    nc = n_pages // ch
    step_of = lambda b, hp, j: (b * grid[1] + hp) * grid[2] + j
    in_specs += [pl.BlockSpec((1, d_attn, page_size), lambda b, hp, j, pt: (step_of(b, hp, j) // nc, 0, 0)),
                 pl.BlockSpec(memory_space=pl.ANY)]
    sc_spec = pl.BlockSpec((1, ch, n_heads, page_size),
                           lambda b, hp, j, pt: (step_of(b, hp, j) // nc, step_of(b, hp, j) % nc, 0, 0))
    sc_shape = jax.ShapeDtypeStruct((dec_seq, n_pages, n_heads, page_size), F32)
    scratch += [pltpu.VMEM((2, ch, n_heads, head_dim, page_size), F32), pltpu.SemaphoreType.DMA((2,))]
    return pl.pallas_call(
        functools.partial(_moba_prompt_kernel, head_dim=head_dim, hps=hps,
                          stream=dict(nc=nc, ch=ch, n_heads=n_heads)),
        grid_spec=pltpu.PrefetchScalarGridSpec(
            num_scalar_prefetch=1, grid=grid, in_specs=in_specs,
            out_specs=[out_specs, sc_spec], scratch_shapes=scratch),
        out_shape=[out_shape, sc_shape], compiler_params=params, name="moba_prompt_stream",
    )(page_table, qaug, kaug, v_tb, qb, ck)


def _sample_scores_kernel(pt_ref, qb_ref, ck_hbm, sc_ref, buf_ref, sem_ref, *, ch, n_heads, head_dim):
    nc = pl.num_programs(1)
    step = pl.program_id(0) * nc + pl.program_id(1)
    total = pl.num_programs(0) * nc
    score_page = _stream_page_scores(step, total, nc, pt_ref, qb_ref, ck_hbm, sc_ref, buf_ref, sem_ref,
                                     ch=ch, n_heads=n_heads, head_dim=head_dim)

    def body(i, carry):
        score_page(i)
        return carry

    lax.fori_loop(0, ch, body, 0)


def _sample_probs_kernel(sc_ref, sself_ref, slopes_ref, p_ref, pself_ref, sel_ref,
                         *, n_pages, page_size, past_len, head_dim, n_heads):
    pages_per_block = MOBA_BLOCK // page_size
    n_blocks = n_pages // pages_per_block
    lanes = sself_ref.shape[-1]
    lane_i = lax.broadcasted_iota(jnp.int32, (n_heads, lanes), 1)
    scale = head_dim ** -0.5
    slope = slopes_ref[...]
    pos_in_page = lax.broadcasted_iota(jnp.int32, (n_heads, page_size), 1)

    def one_sequence(bb):
        gate = jnp.full((n_heads, lanes), NEG_INF, F32)
        for n in range(n_blocks):
            tot = sc_ref[bb, n * pages_per_block]
            for pg in range(1, pages_per_block):
                tot = tot + sc_ref[bb, n * pages_per_block + pg]
            g_n = jnp.sum(tot, axis=1, keepdims=True) * (1.0 / MOBA_BLOCK)
            gate = jnp.where(lane_i == n, g_n, gate)
        picks = []
        for _ in range(MOBA_TOPK):
            mx = jnp.max(gate, axis=1, keepdims=True)
            idx = jnp.min(jnp.where(gate == mx, lane_i, lanes), axis=1, keepdims=True)
            picks.append(idx)
            gate = jnp.where(lane_i == idx, BELOW_NEG_INF, gate)
        sel_vec = jnp.zeros((n_heads, lanes), jnp.int32)
        for t, idx in enumerate(picks):
            sel_vec = jnp.where(lane_i == t, idx, sel_vec)
        sel_ref[bb] = sel_vec

        s_self = sself_ref[bb] * scale

        def page_scores(pg):
            selected = picks[0] == (pg // pages_per_block)
            for idx in picks[1:]:
                selected = selected | (idx == (pg // pages_per_block))
            dist = (past_len - pg * page_size - pos_in_page).astype(F32)
            s = sc_ref[bb, pg] * scale - slope[:, 0:page_size] * dist
            return jnp.where(selected, s, NEG_INF)

        top = page_scores(0)
        for pg in range(1, n_pages):
            top = jnp.maximum(top, page_scores(pg))
        m = jnp.maximum(s_self, jnp.max(top, axis=1, keepdims=True))
        e_self = jnp.exp(s_self - m)
        tot = jnp.zeros((n_heads, page_size), F32)
        for pg in range(n_pages):
            e = jnp.exp(page_scores(pg) - m[:, 0:page_size])
            p_ref[bb, pg] = e
            tot = tot + e
        inv = 1.0 / (e_self + jnp.sum(tot, axis=1, keepdims=True))
        for pg in range(n_pages):
            p_ref[bb, pg] = p_ref[bb, pg] * inv[:, 0:page_size]
        pself_ref[bb] = e_self * inv

    for bb in range(sc_ref.shape[0]):
        one_sequence(bb)


def _sample_pv_kernel(pt_ref, sel_ref, p_ref, pself_e_ref, v_t_ref, cv_hbm, o_ref, vbuf_ref, sem_ref,
                      *, n_heads, head_dim, pages_per_block):
    b = pl.program_id(0)
    nb = pl.num_programs(0)
    per_head = MOBA_TOPK * pages_per_block

    def tile_copy(seq, h, t, pg, slot):
        blk = sel_ref[seq, h * MOBA_TOPK + t]
        page = pt_ref[seq, blk * pages_per_block + pg]
        k = h * per_head + t * pages_per_block + pg
        return pltpu.make_async_copy(cv_hbm.at[page, h], vbuf_ref.at[slot, k], sem_ref.at[slot])

    def for_all_tiles(seq, slot, fn):
        k = 0
        for h in range(n_heads):
            for t in range(MOBA_TOPK):
                for pg in range(pages_per_block):
                    fn(tile_copy(seq, h, t, pg, slot), k)
                    k += 1

    @pl.when(b == 0)
    def _():
        for_all_tiles(b, 0, lambda cp, k: cp.start(priority=k % 2))
        o_ref[...] = jnp.zeros(o_ref.shape, F32)

    @pl.when(b + 1 < nb)
    def _():
        for_all_tiles(b + 1, (b + 1) % 2, lambda cp, k: cp.start(priority=k % 2))

    slot = b % 2
    for_all_tiles(b, slot, lambda cp, k: cp.wait())

    cols = []
    for h in range(n_heads):
        acc = jnp.zeros((head_dim, vbuf_ref.shape[-1]), F32)
        for t in range(MOBA_TOPK):
            blk = sel_ref[b, h * MOBA_TOPK + t]
            for pg in range(pages_per_block):
                k = h * per_head + t * pages_per_block + pg
                p_row = p_ref[0, blk * pages_per_block + pg, h:h + 1, :]
                acc = acc + vbuf_ref[slot, k] * p_row
        cols.append(jnp.sum(acc, axis=1, keepdims=True))
    col = jnp.concatenate(cols, axis=0)
    seq_i = lax.broadcasted_iota(jnp.int32, o_ref.shape, 1)
    o_ref[...] = jnp.where(seq_i == b, col, o_ref[...])

    @pl.when(b == nb - 1)
    def _():
        o_ref[...] = o_ref[...] + pself_e_ref[...] * v_t_ref[...]


def _cache_view(cache):
    return jnp.transpose(cache, (0, 2, 3, 1))


def _stream_chunk(n_steps, dec_seq, n_pages):
    total_pages = dec_seq * n_pages
    if total_pages % n_steps:
        return None
    ch = total_pages // n_steps
    return ch if ch <= PAGES_PER_CHUNK and n_pages % ch == 0 else None


def _sample_scores_call(q_t, ck, page_table, *, n_heads, head_dim):
    d_attn, n_seq = q_t.shape
    page_size = ck.shape[-1]
    n_pages = page_table.shape[1]
    ch = min(PAGES_PER_CHUNK, n_pages)
    assert n_pages % ch == 0
    qb = jnp.broadcast_to(q_t.T[:, :, None], (n_seq, d_attn, page_size))
    return pl.pallas_call(
        functools.partial(_sample_scores_kernel, ch=ch, n_heads=n_heads, head_dim=head_dim),
        grid_spec=pltpu.PrefetchScalarGridSpec(
            num_scalar_prefetch=1,
            grid=(n_seq, n_pages // ch),
            in_specs=[pl.BlockSpec((1, d_attn, page_size), lambda b, c, pt: (b, 0, 0)),
                      pl.BlockSpec(memory_space=pl.ANY)],
            out_specs=pl.BlockSpec((1, ch, n_heads, page_size), lambda b, c, pt: (b, c, 0, 0)),
            scratch_shapes=[pltpu.VMEM((2, ch, n_heads, head_dim, page_size), F32),
                            pltpu.SemaphoreType.DMA((2,))]),
        out_shape=jax.ShapeDtypeStruct((n_seq, n_pages, n_heads, page_size), F32),
        compiler_params=pltpu.CompilerParams(dimension_semantics=("arbitrary", "arbitrary"),
                                             vmem_limit_bytes=VMEM_LIMIT_BYTES),
        name="sample_scores",
    )(page_table, qb, ck)


def _moba_sample(scores, v_t, sself, slopes, cv, page_table, *, n_heads, head_dim):
    d_attn, n_seq = v_t.shape
    page_size = cv.shape[-1]
    n_pages = page_table.shape[1]
    past_len = n_pages * page_size
    pages_per_block = MOBA_BLOCK // page_size
    assert MOBA_BLOCK % page_size == 0 and past_len % MOBA_BLOCK == 0
    assert past_len // MOBA_BLOCK >= MOBA_TOPK

    lanes = 128
    sb = SAMPLE_SEQS_PER_STEP if n_seq % SAMPLE_SEQS_PER_STEP == 0 else 1
    sself_b = jnp.broadcast_to(sself.T[:, :, None], (n_seq, n_heads, lanes))
    slopes_b = jnp.broadcast_to(slopes[:, None], (n_heads, lanes))
    probs, pself, sel = pl.pallas_call(
        functools.partial(_sample_probs_kernel, n_pages=n_pages, page_size=page_size,
                          past_len=past_len, head_dim=head_dim, n_heads=n_heads),
        grid=(n_seq // sb,),
        in_specs=[pl.BlockSpec((sb, n_pages, n_heads, page_size), lambda b: (b, 0, 0, 0)),
                  pl.BlockSpec((sb, n_heads, lanes), lambda b: (b, 0, 0)),
                  pl.BlockSpec((n_heads, lanes), lambda b: (0, 0))],
        out_specs=[pl.BlockSpec((sb, n_pages, n_heads, page_size), lambda b: (b, 0, 0, 0)),
                   pl.BlockSpec((sb, n_heads, lanes), lambda b: (b, 0, 0)),
                   pl.BlockSpec((sb, n_heads, lanes), lambda b: (b, 0, 0))],
        out_shape=[jax.ShapeDtypeStruct((n_seq, n_pages, n_heads, page_size), F32),
                   jax.ShapeDtypeStruct((n_seq, n_heads, lanes), F32),
                   jax.ShapeDtypeStruct((n_seq, n_heads, lanes), jnp.int32)],
        compiler_params=pltpu.CompilerParams(dimension_semantics=("arbitrary",)),
        name="sample_probs",
    )(scores, sself_b, slopes_b)

    sel_flat = sel[:, :, :MOBA_TOPK].reshape(n_seq, n_heads * MOBA_TOPK)
    pself_e = jnp.repeat(pself[:, :, 0].T, head_dim, axis=0)
    n_tiles = n_heads * MOBA_TOPK * pages_per_block
    return pl.pallas_call(
        functools.partial(_sample_pv_kernel, n_heads=n_heads, head_dim=head_dim,
                          pages_per_block=pages_per_block),
        grid_spec=pltpu.PrefetchScalarGridSpec(
            num_scalar_prefetch=2,
            grid=(n_seq,),
            in_specs=[pl.BlockSpec((1, n_pages, n_heads, page_size), lambda b, pt, sl: (b, 0, 0, 0)),
                      pl.BlockSpec((d_attn, n_seq), lambda b, pt, sl: (0, 0)),
                      pl.BlockSpec((d_attn, n_seq), lambda b, pt, sl: (0, 0)),
                      pl.BlockSpec(memory_space=pl.ANY)],
            out_specs=pl.BlockSpec((d_attn, n_seq), lambda b, pt, sl: (0, 0)),
            scratch_shapes=[pltpu.VMEM((2, n_tiles, head_dim, page_size), F32),
                            pltpu.SemaphoreType.DMA((2,))]),
        out_shape=jax.ShapeDtypeStruct((d_attn, n_seq), F32),
        compiler_params=pltpu.CompilerParams(dimension_semantics=("arbitrary",)),
        name="sample_pv",
    )(page_table, sel_flat, probs, pself_e, v_t, cv)


def _post_kernel(attn_ref, siga_ref, gconv_ref, x_ref, p_ref, wao_ref, wo_ref, n2_ref, wg_ref, wu_ref,
                 wd_ref, n3_ref, wple_ref, wpg_ref, y_ref, *, ff_chunks):
    y_attn = _dot(attn_ref[...], wao_ref[...])
    merged = siga_ref[...].astype(F32) * y_attn + gconv_ref[...].astype(F32)
    x1 = x_ref[...] + _dot(merged.astype(BF16), wo_ref[...])
    h2 = _rms_norm(x1, n2_ref[...]).astype(BF16)
    x2 = x1
    for lo, hi in ff_chunks:
        g = _dot(h2, wg_ref[:, lo:hi])
        u = _dot(h2, wu_ref[:, lo:hi])
        a = (g * jax.nn.sigmoid(g) * u).astype(BF16)
        x2 = x2 + _dot(a, wd_ref[lo:hi, :])
    h3 = _rms_norm(x2, n3_ref[...]).astype(BF16)
    ple = _dot(p_ref[...].astype(BF16), wple_ref[...])
    y_ref[...] = x2 + ple * jax.nn.sigmoid(_dot(h3, wpg_ref[...]))


def _post_call(attn, siga, gconv, x2d, p2d, weights, *, tm):
    (wao, wo, n2, wg, wu, wd, n3, wple, wpg) = weights
    rows, d_model = x2d.shape
    d_ff = wg.shape[1]
    chunk = 1024
    ff_chunks = tuple((lo, min(lo + chunk, d_ff)) for lo in range(0, d_ff, chunk))
    row = lambda w: pl.BlockSpec((tm, w), lambda i: (i, 0))
    return pl.pallas_call(
        functools.partial(_post_kernel, ff_chunks=ff_chunks),
        grid=(rows // tm,),
        in_specs=[row(attn.shape[1]), row(d_model), row(d_model), row(d_model), row(p2d.shape[1])]
                 + [_const_spec(w.shape) for w in weights],
        out_specs=row(d_model),
        out_shape=jax.ShapeDtypeStruct((rows, d_model), F32),
        compiler_params=pltpu.CompilerParams(dimension_semantics=("arbitrary",),
                                             vmem_limit_bytes=VMEM_LIMIT_BYTES),
        name="post",
    )(attn, siga, gconv, x2d, p2d, *weights)


def kernel(x_prompt, x_sample, p_prompt, p_sample, cache_k, cache_v, state_conv, page_table, norm1, w_in, q_norm, k_norm, conv_w, w_attn_out, w_conv_out, w_o, norm2, w_gate, w_up, w_down, norm3, w_ple, w_ple_gate):
    depth = w_in.shape[0]
    n_seq, seq_len, d_model = x_prompt.shape
    dec_seq, dec_len, _ = x_sample.shape
    head_dim = q_norm.shape[-1]
    d_attn = w_attn_out.shape[1]
    d_conv = w_conv_out.shape[1]
    n_heads = d_attn // head_dim
    n_blocks = seq_len // MOBA_BLOCK
    assert dec_len == 1, "sample path handles one new token per sequence"
    assert conv_w.shape[1] == CONV_WIDTH
    assert seq_len % ROW_TILE == 0 and ROW_TILE % MOBA_BLOCK == 0
    assert MOBA_BLOCK == 1 << MOBA_BLOCK_LOG2
    assert n_blocks <= AUG_MAX_BLOCKS and AUG_END <= head_dim and head_dim < V_ROWS
    slopes = 2.0 ** (-8.0 * jnp.arange(1, n_heads + 1, dtype=F32) / n_heads)

    yp = x_prompt
    ys = x_sample.reshape(dec_seq, d_model)
    outs = [[] for _ in range(6)]
    for i in range(depth):
        proj_w = (norm1[i][None, :],
                  w_in[i][:, :3 * d_attn].T.astype(BF16),
                  w_in[i][:, 3 * d_attn:].astype(BF16),
                  jnp.tile(q_norm[i], n_heads)[:, None],
                  jnp.tile(k_norm[i], n_heads)[:, None],
                  conv_w[i], w_conv_out[i].astype(BF16), n_heads, head_dim)
        post_w = (w_attn_out[i].astype(BF16), w_o[i].astype(BF16), norm2[i][None, :],
                  w_gate[i].astype(BF16), w_up[i].astype(BF16), w_down[i].astype(BF16),
                  norm3[i][None, :], w_ple[i].astype(BF16), w_ple_gate[i].astype(BF16))

        hist = state_conv[i]
        sq_t, sk_t, sv_t, sself, sgconv, ssiga, su = _proj_call(
            ys[None], (hist[:, 0], hist[:, 1]), proj_w, seq_mode=False, tm=dec_seq)
        ck = _cache_view(cache_k[i])
        n_pages = page_table.shape[1]
        attn_grid = _attn_grid(n_seq, seq_len, n_heads)
        ch = _stream_chunk(attn_grid[0] * attn_grid[1] * attn_grid[2], dec_seq, n_pages)

        k_t, v_t, v_tb, kaug, qaug, gconv, siga, tails = _proj_call(
            yp, None, proj_w, seq_mode=True, tm=ROW_TILE, slopes=slopes)
        if ch is None:
            attn = _moba_prompt_call(qaug, kaug, v_tb, n_heads=n_heads, head_dim=head_dim)
            scores = _sample_scores_call(sq_t[0], ck, page_table, n_heads=n_heads, head_dim=head_dim)
        else:
            qb = jnp.broadcast_to(sq_t[0].T[:, :, None], (dec_seq, d_attn, ck.shape[-1]))
            attn, scores = _moba_prompt_call(qaug, kaug, v_tb, n_heads=n_heads, head_dim=head_dim,
                                             stream=(page_table, qb, ck, ch))
        yp = _post_call(attn.reshape(n_seq * seq_len, d_attn), siga, gconv,
                        yp.reshape(n_seq * seq_len, d_model),
                        p_prompt[i].reshape(n_seq * seq_len, -1), post_w,
                        tm=ROW_TILE).reshape(n_seq, seq_len, d_model)
        to_cache_layout = lambda t, n, l: jnp.transpose(
            t.reshape(n, n_heads, head_dim, l), (0, 3, 1, 2))
        outs[0].append(to_cache_layout(k_t, n_seq, seq_len))
        outs[1].append(to_cache_layout(v_t, n_seq, seq_len))
        tiles_per_seq = seq_len // ROW_TILE
        outs[2].append(tails.reshape(n_seq, tiles_per_seq, CONV_WIDTH - 1, d_conv)[:, -1])

        o_t = _moba_sample(scores, sv_t[0], sself, slopes, _cache_view(cache_v[i]), page_table,
                           n_heads=n_heads, head_dim=head_dim)
        ys = _post_call(o_t.T.astype(BF16), ssiga, sgconv, ys, p_sample[i].reshape(dec_seq, -1),
                        post_w, tm=dec_seq)
        outs[3].append(to_cache_layout(sk_t, 1, dec_seq).reshape(dec_seq, 1, n_heads, head_dim))
        outs[4].append(to_cache_layout(sv_t, 1, dec_seq).reshape(dec_seq, 1, n_heads, head_dim))
        outs[5].append(jnp.stack([hist[:, 1], su], axis=1))

    return (yp, ys.reshape(dec_seq, 1, d_model), *[jnp.stack(o) for o in outs])
```
